```python
import jax, jax.numpy as jnp
from jax import lax
import numpy as np

D_MODEL = 1024
BATCH = 8
SEQ = 2048
DEPTH = 2

HEAD_DIM = 64
N_HEADS = D_MODEL // HEAD_DIM
N_SB_HEADS = N_HEADS // 2
N_DSA_HEADS = N_HEADS - N_SB_HEADS
SB_WIDTH = N_SB_HEADS * HEAD_DIM
DSA_WIDTH = N_DSA_HEADS * HEAD_DIM
MIX_WIDTH = SB_WIDTH + DSA_WIDTH
KV_LORA = D_MODEL // 4
N_IDX_HEADS = 4
IDX_DIM = 64
TOPK_MAX = 256
Q_BLOCK = 128
D_FF = -(-8 * D_MODEL // (3 * 256)) * 256
RMS_EPS = 1e-6
IN_SIZES = (SB_WIDTH, SB_WIDTH, SB_WIDTH,
            DSA_WIDTH, KV_LORA,
            N_IDX_HEADS * IDX_DIM, IDX_DIM,
            N_IDX_HEADS)
IN_COLS = sum(IN_SIZES)

kernel_name = "hybrid_stickbreak_dsa_adaln_block"


def _rmsnorm(x, gain):
    xf = x.astype(jnp.float32)
    y = xf * lax.rsqrt(jnp.mean(xf * xf, axis=-1, keepdims=True) + RMS_EPS)
    return (y * gain.astype(jnp.float32)).astype(x.dtype)


def _modulate(h, shift, scale):
    return h * (1.0 + scale[:, None, :]) + shift[:, None, :]


def _alibi_slopes(n):
    return 2.0 ** (-8.0 * jnp.arange(1, n + 1, dtype=jnp.float32) / n)


def _stick_breaking_attention(q, k, v):
    B, S, H, dh = q.shape
    scale = dh ** -0.5
    outs = []
    for i in range(S // Q_BLOCK):
        q0, end = i * Q_BLOCK, (i + 1) * Q_BLOCK
        z = jnp.einsum('bthd,bshd->bhts', q[:, q0:end], k[:, :end]).astype(jnp.float32) * scale
        t_pos = q0 + jnp.arange(Q_BLOCK)[:, None]
        s_pos = jnp.arange(end)[None, :]
        causal = s_pos < t_pos
        log_1m_beta = jnp.where(causal, jax.nn.log_sigmoid(-z), 0.0)
        tail = lax.cumsum(log_1m_beta, axis=3, reverse=True) - log_1m_beta
        A = jnp.where(causal, jnp.exp(jax.nn.log_sigmoid(z) + tail), 0.0)
        outs.append(jnp.einsum('bhts,bshd->bthd', A.astype(v.dtype), v[:, :end]))
    return jnp.concatenate(outs, axis=1)


def _dsa_attention(q, k, v, q_idx, k_idx, w_idx, slopes, topk):
    B, S, H, dh = q.shape
    scale = dh ** -0.5
    idx_q_scale = IDX_DIM ** -0.5
    idx_w_scale = N_IDX_HEADS ** -0.5
    b_idx = jnp.arange(B)[:, None, None]
    s_pos = jnp.arange(S)

    def block(i):
        q0 = i * Q_BLOCK
        qb = lax.dynamic_slice_in_dim(q, q0, Q_BLOCK, axis=1)
        qib = lax.dynamic_slice_in_dim(q_idx, q0, Q_BLOCK, axis=1)
        wib = lax.dynamic_slice_in_dim(w_idx, q0, Q_BLOCK, axis=1)
        t_pos = q0 + jnp.arange(Q_BLOCK)
        causal = s_pos[None, :] <= t_pos[:, None]
        idx_logits = jnp.einsum('bthd,bsd->btsh', qib, k_idx).astype(jnp.float32) * idx_q_scale
        score = jnp.einsum('btsh,bth->bts', jax.nn.relu(idx_logits),
                           wib.astype(jnp.float32) * idx_w_scale)
        score = jnp.where(causal[None], score, -jnp.inf)
        _, sel = lax.top_k(score, topk)
        k_sel = k[b_idx, sel]
        v_sel = v[b_idx, sel]
        valid = sel <= t_pos[None, :, None]
        dist = (t_pos[None, :, None] - sel).astype(jnp.float32)
        logits = jnp.einsum('bthd,btkd->bhtk', qb, k_sel).astype(jnp.float32) * scale
        logits = logits - slopes[None, :, None, None] * dist[:, None]
        logits = jnp.where(valid[:, None], logits, -jnp.inf)
        p = jax.nn.softmax(logits, axis=-1)
        return jnp.einsum('bhtk,btkd->bthd', p.astype(v.dtype), v_sel)

    out = lax.map(block, jnp.arange(S // Q_BLOCK))
    return jnp.moveaxis(out, 0, 1).reshape(B, S, H, dh)


def _hybrid_mixer(h, w_in, kv_gain, w_uk, w_uv, sb_out_gain, dsa_out_gain, w_o, slopes, topk):
    B, S, _ = h.shape
    proj = h @ w_in
    sb_q, sb_k, sb_v, dsa_q, kv_lat, idx_q, idx_k, idx_w = jnp.split(
        proj, np.cumsum(IN_SIZES)[:-1], axis=-1)
    sb_heads = lambda t: t.reshape(B, S, N_SB_HEADS, HEAD_DIM)
    o_sb = _stick_breaking_attention(sb_heads(sb_q), sb_heads(sb_k), sb_heads(sb_v))
    kv = _rmsnorm(kv_lat, kv_gain)
    k_d = kv @ w_uk
    v_d = kv @ w_uv
    o_dsa = _dsa_attention(dsa_q.reshape(B, S, N_DSA_HEADS, HEAD_DIM), k_d, v_d,
                           idx_q.reshape(B, S, N_IDX_HEADS, IDX_DIM), idx_k, idx_w,
                           slopes, topk)
    o = jnp.concatenate([_rmsnorm(o_sb.reshape(B, S, SB_WIDTH), sb_out_gain),
                         _rmsnorm(o_dsa.reshape(B, S, DSA_WIDTH), dsa_out_gain)], axis=-1)
    return o @ w_o


def _swiglu(h, w_gate, w_up, w_down):
    return (jax.nn.silu(h @ w_gate) * (h @ w_up)) @ w_down


def setup_inputs(seed: int = 0) -> dict:
    key = jax.random.key(seed)
    ks = jax.random.split(key, 20)
    nrm = lambda k, shape, s: jax.random.normal(k, shape, jnp.float32) * s
    gain = lambda k, shape: 1.0 + nrm(k, shape, 0.02)
    L, D = DEPTH, D_MODEL
    return {
        "x": nrm(ks[0], (BATCH, SEQ, D), 1.0),
        "c": nrm(ks[1], (BATCH, D), 1.0),
        "w_mod": nrm(ks[2], (L, D, 6 * D), 0.5 * D ** -0.5),
        "b_mod": nrm(ks[3], (L, 6 * D), 0.02),
        "norm1_gain": gain(ks[4], (L, D)),
        "norm2_gain": gain(ks[5], (L, D)),
        "w_in": nrm(ks[6], (L, D, IN_COLS), D ** -0.5),
        "kv_gain": gain(ks[7], (L, KV_LORA)),
        "w_uk": nrm(ks[8], (L, KV_LORA, HEAD_DIM), KV_LORA ** -0.5),
        "w_uv": nrm(ks[9], (L, KV_LORA, HEAD_DIM), KV_LORA ** -0.5),
        "sb_out_gain": gain(ks[10], (L, SB_WIDTH)),
        "dsa_out_gain": gain(ks[11], (L, DSA_WIDTH)),
        "w_o": nrm(ks[12], (L, MIX_WIDTH, D), MIX_WIDTH ** -0.5),
        "w_gate": nrm(ks[13], (L, D, D_FF), D ** -0.5),
        "w_up": nrm(ks[14], (L, D, D_FF), D ** -0.5),
        "w_down": nrm(ks[15], (L, D_FF, D), D_FF ** -0.5),
        "w_mod_final": nrm(ks[16], (D, 2 * D), 0.5 * D ** -0.5),
        "b_mod_final": nrm(ks[17], (2 * D,), 0.02),
        "final_gain": gain(ks[18], (D,)),
    }


def reference(x, c, w_mod, b_mod, norm1_gain, norm2_gain, w_in, kv_gain, w_uk, w_uv,
              sb_out_gain, dsa_out_gain, w_o, w_gate, w_up, w_down,
              w_mod_final, b_mod_final, final_gain):
    S = x.shape[1]
    topk = min(TOPK_MAX, S // 4)
    slopes = _alibi_slopes(N_DSA_HEADS)
    c_act = jax.nn.silu(c)
    for l in range(DEPTH):
        mod = c_act @ w_mod[l] + b_mod[l]
        sh1, sc1, g1, sh2, sc2, g2 = jnp.split(mod, 6, axis=-1)
        h = _modulate(_rmsnorm(x, norm1_gain[l]), sh1, sc1)
        x = x + g1[:, None, :] * _hybrid_mixer(h, w_in[l], kv_gain[l], w_uk[l], w_uv[l],
                                               sb_out_gain[l], dsa_out_gain[l], w_o[l],
                                               slopes, topk)
        h = _modulate(_rmsnorm(x, norm2_gain[l]), sh2, sc2)
        x = x + g2[:, None, :] * _swiglu(h, w_gate[l], w_up[l], w_down[l])
    sh_f, sc_f = jnp.split(c_act @ w_mod_final + b_mod_final, 2, axis=-1)
    return _modulate(_rmsnorm(x, final_gain), sh_f, sc_f)
```

```python
import functools

import jax
import jax.numpy as jnp
from jax import lax
from jax.experimental import pallas as pl
from jax.experimental.pallas import tpu as pltpu

F32 = jnp.float32
BF16 = jnp.bfloat16
I32 = jnp.int32

HEAD_DIM = 64
LANES = 128
Q_BLOCK = 128
N_IDX_HEADS = 4
IDX_DIM = 64
TOPK_MAX = 256
RMS_EPS = 1e-6
INT_MIN = -2 ** 31
NEG_BIG = -1e30
VMEM_LIMIT = 52 * 1024 * 1024


def _nt(a, b):
    return lax.dot_general(a, b, (((1,), (1,)), ((), ())), preferred_element_type=F32)


def _dot(a, b):
    return jnp.dot(a, b, preferred_element_type=F32)


def _split(x):
    hi = x.astype(BF16)
    lo = (x - hi.astype(F32)).astype(BF16)
    return hi, lo


def _rms(x, gain):
    ms = jnp.mean(x * x, axis=-1, keepdims=True)
    return x * lax.rsqrt(ms + RMS_EPS) * gain


def _mod_kernel(c_ref, w_ref, b_ref, o_ref):
    c = c_ref[...]
    ca = c / (1.0 + jnp.exp(-c))
    ch, cl = _split(ca)
    wh, wl = _split(w_ref[0])
    o_ref[0] = _dot(ch, wh) + _dot(ch, wl) + _dot(cl, wh) + b_ref[0]


def _mod(c, w, b, tn=1024):
    nl, d, n = w.shape
    bsz = c.shape[0]
    return pl.pallas_call(
        _mod_kernel,
        grid=(nl, n // tn),
        in_specs=[pl.BlockSpec((bsz, d), lambda l, j: (0, 0)),
                  pl.BlockSpec((1, d, tn), lambda l, j: (l, 0, j)),
                  pl.BlockSpec((1, 1, tn), lambda l, j: (l, 0, j))],
        out_specs=pl.BlockSpec((1, bsz, tn), lambda l, j: (l, 0, j)),
        out_shape=jax.ShapeDtypeStruct((nl, bsz, n), F32),
        name="adaln_mod",
    )(c, w, b.reshape(nl, 1, n))


def _in_kernel(x_ref, mod_ref, g_ref, wa_ref, wih_ref, wil_ref, kvg_ref, wkv_ref, bkv_ref,
               sbq_ref, sbk_ref, sbv_ref, dq_ref, kd2_ref, vde_ref, vdo_ref,
               qih_ref, qil_ref, ki_ref, wt_ref, *, sb_w, dsa_w, kv_w):
    x = x_ref[0]
    mod = mod_ref[0]
    h = _rms(x, g_ref[...]) * (1.0 + mod[1:2]) + mod[0:1]
    hh, hl = _split(h)
    main = _dot(hh, wa_ref[...])
    scale = HEAD_DIM ** -0.5
    sbq_ref[0] = (main[:, 0:sb_w] * scale).astype(BF16)
    sbk_ref[0] = main[:, sb_w:2 * sb_w].astype(BF16)
    sbv_ref[0] = main[:, 2 * sb_w:3 * sb_w].astype(BF16)
    o = 3 * sb_w
    dq_ref[0] = (main[:, o:o + dsa_w] * scale).astype(BF16)
    o += dsa_w
    kv = _rms(main[:, o:o + kv_w], kvg_ref[...])
    kvp = _dot(kv.astype(BF16), wkv_ref[...]) + bkv_ref[...]
    kd2_ref[0] = kvp[:, 0:LANES].astype(BF16)
    vde_ref[0] = kvp[:, LANES:3 * LANES].astype(BF16)
    vdo_ref[0] = kvp[:, 3 * LANES:5 * LANES].astype(BF16)
    wih = wih_ref[...]
    idx = _dot(hh, wih) + _dot(hh, wil_ref[...]) + _dot(hl, wih)
    nq = N_IDX_HEADS * IDX_DIM
    qh, ql = _split(idx[:, 0:nq] * (IDX_DIM ** -0.5))
    qih_ref[0] = qh
    qil_ref[0] = ql
    kw = idx[:, nq:nq + LANES]
    lane = lax.broadcasted_iota(I32, kw.shape, 1)
    first = lane < IDX_DIM
    ke = jnp.where(first, kw, 0.0)
    ko = jnp.where(first, 0.0, pltpu.roll(kw, IDX_DIM, 1))
    keh, kel = _split(ke)
    koh, kol = _split(ko)
    ki_ref[0, :, 0:LANES] = keh
    ki_ref[0, :, LANES:2 * LANES] = koh
    ki_ref[0, :, 2 * LANES:3 * LANES] = kel
    ki_ref[0, :, 3 * LANES:4 * LANES] = kol
    kwt = kw.T
    wt_ref[0] = kwt[IDX_DIM:IDX_DIM + 8, :] * (N_IDX_HEADS ** -0.5)


def _in_proj(x, mod, gain, wa, wih, wil, kvg, wkv, bkv, *, sb_w, dsa_w, kv_w, tm=512):
    bsz, s, d = x.shape
    na = wa.shape[1]
    ni = wih.shape[1]
    nkv = wkv.shape[1]
    full = lambda shape: pl.BlockSpec(shape, lambda b, i: (0,) * len(shape))
    row = lambda w: pl.BlockSpec((1, tm, w), lambda b, i: (b, i, 0))
    nq = N_IDX_HEADS * IDX_DIM
    outs = [
        (sb_w, BF16), (sb_w, BF16), (sb_w, BF16), (dsa_w, BF16),
        (LANES, BF16), (2 * LANES, BF16), (2 * LANES, BF16),
        (nq, BF16), (nq, BF16), (4 * LANES, BF16),
    ]
    out_shape = [jax.ShapeDtypeStruct((bsz, s, w), dt) for w, dt in outs]
    out_specs = [row(w) for w, _ in outs]
    out_shape.append(jax.ShapeDtypeStruct((bsz, 8, s), F32))
    out_specs.append(pl.BlockSpec((1, 8, tm), lambda b, i: (b, 0, i)))
    return pl.pallas_call(
        functools.partial(_in_kernel, sb_w=sb_w, dsa_w=dsa_w, kv_w=kv_w),
        grid=(bsz, s // tm),
        in_specs=[row(d),
                  pl.BlockSpec((1, 6, d), lambda b, i: (b, 0, 0)),
                  full((1, d)), full((d, na)), full((d, ni)), full((d, ni)),
                  full((1, kv_w)), full((kv_w, nkv)), full((1, nkv))],
        out_specs=out_specs,
        out_shape=out_shape,
        compiler_params=pltpu.CompilerParams(
            dimension_semantics=("parallel", "parallel"), vmem_limit_bytes=VMEM_LIMIT),
        name="in_proj",
    )(x, mod, gain, wa, wih, wil, kvg, wkv, bkv)


def _sb_kernel(q_ref, k_ref, v_ref, o_ref, acc_ref, car_ref):
    i = pl.program_id(2)
    t = Q_BLOCK
    q2 = q_ref[0]
    lane = lax.broadcasted_iota(I32, (t, LANES), 1)
    row = lax.broadcasted_iota(I32, (t, LANES), 0)
    first = lane < HEAD_DIM
    zb = jnp.zeros((t, LANES), BF16)
    qh = (jnp.where(first, q2, zb), jnp.where(first, zb, q2))
    r2 = lax.broadcasted_iota(I32, (LANES, 2 * LANES), 0)
    c2 = lax.broadcasted_iota(I32, (LANES, 2 * LANES), 1)
    uo = jnp.where((c2 >= LANES) | (r2 > c2), 1.0, 0.0).astype(BF16)
    strict = lane < row

    def chunk(c, diag):
        start = pl.multiple_of(c * t, t)
        k2 = k_ref[0, pl.ds(start, t), :]
        v2 = v_ref[0, pl.ds(start, t), :]
        vh = (jnp.where(first, v2, zb), jnp.where(first, zb, v2))
        pv = None
        for hd in range(2):
            z = _nt(qh[hd], k2)
            sp = jnp.maximum(z, 0.0) + jnp.log(1.0 + jnp.exp(-jnp.abs(z)))
            l1m = jnp.where(strict, -sp, 0.0) if diag else -sp
            hi, lo = _split(l1m)
            cs = _dot(hi, uo) + _dot(lo, uo)
            if diag:
                tail = cs[:, :LANES]
                car_ref[hd] = cs[:, LANES:]
            else:
                tail = cs[:, :LANES] + car_ref[hd]
                car_ref[hd] += cs[:, LANES:]
            a = jnp.exp(z - sp + tail)
            if diag:
                a = jnp.where(strict, a, 0.0)
            d = _dot(a.astype(BF16), vh[hd])
            pv = d if pv is None else pv + d
        if diag:
            acc_ref[...] = pv
        else:
            acc_ref[...] += pv

    chunk(i, True)

    def body(j, carry):
        chunk(i - 1 - j, False)
        return carry

    lax.fori_loop(0, i, body, 0)
    o_ref[0] = acc_ref[...]


def _sb_attention(q, k, v):
    bsz, s, w = q.shape
    npair = w // LANES
    nq = s // Q_BLOCK
    return pl.pallas_call(
        _sb_kernel,
        grid=(bsz, npair, nq),
        in_specs=[pl.BlockSpec((1, Q_BLOCK, LANES), lambda b, p, i: (b, i, p)),
                  pl.BlockSpec((1, s, LANES), lambda b, p, i: (b, 0, p)),
                  pl.BlockSpec((1, s, LANES), lambda b, p, i: (b, 0, p))],
        out_specs=pl.BlockSpec((1, Q_BLOCK, LANES), lambda b, p, i: (b, i, p)),
        out_shape=jax.ShapeDtypeStruct((bsz, s, w), F32),
        scratch_shapes=[pltpu.VMEM((Q_BLOCK, LANES), F32),
                        pltpu.VMEM((2, Q_BLOCK, LANES), F32)],
        compiler_params=pltpu.CompilerParams(
            dimension_semantics=("parallel", "parallel", "arbitrary")),
        name="sb_attention",
    )(q, k, v)


def _sortable(x):
    b = lax.bitcast_convert_type(x + 0.0, I32)
    return b ^ ((b >> 31) & 0x7FFFFFFF)


def _dsa_kernel(dq_ref, kd2_ref, vde_ref, vdo_ref, qih_ref, qil_ref, ki_ref, wt_ref, o_ref,
                key_ref, bias_ref, *, topk, n_heads):
    i = pl.program_id(1)
    t = Q_BLOCK
    nchunk = i + 1
    kf = float(topk)
    lane = lax.broadcasted_iota(I32, (t, LANES), 1)
    row = lax.broadcasted_iota(I32, (t, LANES), 0)
    first = lane < HEAD_DIM

    qih = qih_ref[0]
    qil = qil_ref[0]
    wt = wt_ref[0]

    def score_chunk(c, carry):
        start = pl.multiple_of(c * t, t)
        kc = ki_ref[0, pl.ds(start, t), :]
        score = jnp.zeros((t, LANES), F32)
        for pr in range(N_IDX_HEADS // 2):
            qh = qih[:, pr * LANES:(pr + 1) * LANES]
            ql = qil[:, pr * LANES:(pr + 1) * LANES]
            for half in range(2):
                kh = kc[:, half * LANES:(half + 1) * LANES]
                kl = kc[:, (2 + half) * LANES:(3 + half) * LANES]
                lg = _nt(kh, qh) + _nt(kh, ql) + _nt(kl, qh)
                hd = 2 * pr + half
                score = score + jnp.maximum(lg, 0.0) * wt[hd:hd + 1, :]
        key = _sortable(score)
        noncausal = (c == i) & (row > lane)
        key_ref[pl.ds(start, t), :] = jnp.where(noncausal, INT_MIN, key)
        return carry

    lax.fori_loop(0, nchunk, score_chunk, 0)

    def count(pred):
        def body(c, acc):
            start = pl.multiple_of(c * t, t)
            m = jnp.where(pred(key_ref[pl.ds(start, t), :]), 1.0, 0.0)
            return acc + m.reshape(t // 8, 8, LANES).sum(axis=0)
        part = lax.fori_loop(0, nchunk, body, jnp.zeros((8, LANES), F32))
        return jnp.sum(part, axis=0, keepdims=True)

    def bit_step(bi, lo):
        cand = lo + lax.shift_left(jnp.int32(1), 31 - bi)
        cnt = count(lambda blk: blk >= cand)
        return jnp.where(cnt >= kf, cand, lo)

    thr = lax.fori_loop(0, 32, bit_step, jnp.full((1, LANES), INT_MIN, I32))
    cnt_ge = count(lambda blk: blk >= thr)

    @pl.when(jnp.max(cnt_ge) > kf)
    def _():
        need = kf - count(lambda blk: blk > thr)
        sl = jnp.where(lane < row, 1.0, 0.0).astype(BF16)

        def body(c, before):
            start = pl.multiple_of(c * t, t)
            blk = key_ref[pl.ds(start, t), :]
            eq = blk == thr
            eqf = jnp.where(eq, 1.0, 0.0)
            rank = _dot(sl, eqf.astype(BF16)) + before
            key_ref[pl.ds(start, t), :] = jnp.where(eq & (rank >= need), INT_MIN, blk)
            return before + jnp.sum(eqf, axis=0, keepdims=True)

        lax.fori_loop(0, nchunk, body, jnp.zeros((1, LANES), F32))

    thr_eff = jnp.maximum(thr, INT_MIN + 1)
    thr_col = jnp.broadcast_to(thr_eff, (t, LANES)).T

    def bias_chunk(c, carry):
        start = pl.multiple_of(c * t, t)
        kt = key_ref[pl.ds(start, t), :].T
        bias_ref[:, pl.ds(start, t)] = jnp.where(kt >= thr_col, 0.0, -jnp.inf)
        return carry

    lax.fori_loop(0, nchunk, bias_chunk, 0)

    n256 = (i + 2) // 2

    @pl.when(nchunk < 2 * n256)
    def _():
        start = pl.multiple_of(nchunk * t, t)
        bias_ref[:, pl.ds(start, t)] = jnp.full((t, LANES), -jnp.inf, F32)

    ck = 2 * LANES
    rc = (lax.broadcasted_iota(I32, (t, ck), 0) - lax.broadcasted_iota(I32, (t, ck), 1)).astype(F32)
    lane2 = lax.broadcasted_iota(I32, (t, ck), 1)
    even_lanes = (lane2 % LANES) < HEAD_DIM
    zb = jnp.zeros((t, LANES), BF16)
    q0 = i * t
    for pr in range(n_heads // 2):
        q2 = dq_ref[0, :, pr * LANES:(pr + 1) * LANES]
        qs = (jnp.where(first, q2, zb), jnp.where(first, zb, q2))
        slopes = tuple(2.0 ** (-8.0 * (2 * pr + hd + 1) / n_heads) for hd in range(2))

        def body(cc, carry):
            m_e, m_o, acc = carry
            start = pl.multiple_of(cc * ck, ck)
            kd = kd2_ref[0, pl.ds(start, ck), :]
            vs = (vde_ref[0, pl.ds(start, ck), :], vdo_ref[0, pl.ds(start, ck), :])
            bias = bias_ref[:, pl.ds(start, ck)]
            dist = rc + (q0 - cc * ck).astype(F32)
            ms = (m_e, m_o)
            new_m, alphas, pv = [], [], None
            for hd in range(2):
                sc = _nt(qs[hd], kd) - slopes[hd] * dist + bias
                m_new = jnp.maximum(ms[hd], jnp.max(sc, axis=1, keepdims=True))
                alphas.append(jnp.exp(ms[hd] - m_new))
                p = jnp.exp(sc - m_new)
                d = _dot(p.astype(BF16), vs[hd])
                pv = d if pv is None else pv + d
                new_m.append(m_new)
            alpha = jnp.where(even_lanes, alphas[0], alphas[1])
            return new_m[0], new_m[1], acc * alpha + pv

        init = (jnp.full((t, 1), NEG_BIG, F32), jnp.full((t, 1), NEG_BIG, F32),
                jnp.zeros((t, ck), F32))
        _, _, acc = lax.fori_loop(0, n256, body, init)
        o_ref[0, :, pr * LANES:(pr + 1) * LANES] = acc[:, :LANES] / acc[:, LANES:]


def _dsa_attention(dq, kd2, vde, vdo, qih, qil, ki, wt, *, topk):
    bsz, s, w = dq.shape
    nq = s // Q_BLOCK
    blk = lambda wd: pl.BlockSpec((1, Q_BLOCK, wd), lambda b, i: (b, i, 0))
    full = lambda wd: pl.BlockSpec((1, s, wd), lambda b, i: (b, 0, 0))
    return pl.pallas_call(
        functools.partial(_dsa_kernel, topk=topk, n_heads=w // HEAD_DIM),
        grid=(bsz, nq),
        in_specs=[blk(w), full(LANES), full(2 * LANES), full(2 * LANES),
                  blk(qih.shape[2]), blk(qil.shape[2]), full(4 * LANES),
                  pl.BlockSpec((1, 8, Q_BLOCK), lambda b, i: (b, 0, i))],
        out_specs=blk(w),
        out_shape=jax.ShapeDtypeStruct((bsz, s, w), F32),
        scratch_shapes=[pltpu.VMEM((s, LANES), I32),
                        pltpu.VMEM((Q_BLOCK, s), F32)],
        compiler_params=pltpu.CompilerParams(
            dimension_semantics=("parallel", "arbitrary"), vmem_limit_bytes=VMEM_LIMIT),
        name="dsa_attention",
    )(dq, kd2, vde, vdo, qih, qil, ki, wt)


def _out_kernel(x_ref, osb_ref, odsa_ref, mod_ref, gsb_ref, gdsa_ref, wo_ref, o_ref, *, sb_w):
    a = _rms(osb_ref[0], gsb_ref[...]).astype(BF16)
    b = _rms(odsa_ref[0], gdsa_ref[...]).astype(BF16)
    y = _dot(a, wo_ref[0:sb_w, :]) + _dot(b, wo_ref[sb_w:, :])
    o_ref[0] = x_ref[0] + mod_ref[0][2:3] * y


def _out_proj(x, osb, odsa, mod, gsb, gdsa, wo, tm=512):
    bsz, s, d = x.shape
    sb_w, dsa_w = osb.shape[2], odsa.shape[2]
    row = lambda w: pl.BlockSpec((1, tm, w), lambda b, i: (b, i, 0))
    full = lambda shape: pl.BlockSpec(shape, lambda b, i: (0,) * len(shape))
    return pl.pallas_call(
        functools.partial(_out_kernel, sb_w=sb_w),
        grid=(bsz, s // tm),
        in_specs=[row(d), row(sb_w), row(dsa_w),
                  pl.BlockSpec((1, 6, d), lambda b, i: (b, 0, 0)),
                  full((1, sb_w)), full((1, dsa_w)), full((sb_w + dsa_w, d))],
        out_specs=row(d),
        out_shape=jax.ShapeDtypeStruct((bsz, s, d), F32),
        compiler_params=pltpu.CompilerParams(
            dimension_semantics=("parallel", "parallel"), vmem_limit_bytes=VMEM_LIMIT),
        name="out_proj",
    )(x, osb, odsa, mod, gsb, gdsa, wo)


def _ffn_kernel(x_ref, mod_ref, g_ref, wg_ref, wu_ref, wd_ref, fg_ref, fmod_ref, o_ref,
                h_ref, acc_ref, *, final):
    j = pl.program_id(2)

    @pl.when(j == 0)
    def _():
        mod = mod_ref[0]
        h = _rms(x_ref[0], g_ref[...]) * (1.0 + mod[4:5]) + mod[3:4]
        h_ref[...] = h.astype(BF16)

    h = h_ref[...]
    g = _dot(h, wg_ref[...])
    u = _dot(h, wu_ref[...])
    act = (g / (1.0 + jnp.exp(-g))) * u
    part = _dot(act.astype(BF16), wd_ref[...])

    @pl.when(j == 0)
    def _():
        acc_ref[...] = part

    @pl.when(j > 0)
    def _():
        acc_ref[...] += part

    @pl.when(j == pl.num_programs(2) - 1)
    def _():
        y = x_ref[0] + mod_ref[0][5:6] * acc_ref[...]
        if final:
            fmod = fmod_ref[0]
            y = _rms(y, fg_ref[...]) * (1.0 + fmod[1:2]) + fmod[0:1]
        o_ref[0] = y


def _ffn(x, mod, gain, wg, wu, wd, fgain, fmod, *, final, tm=1024, tf=256):
    bsz, s, d = x.shape
    dff = wg.shape[1]
    tm = min(tm, s)
    row = pl.BlockSpec((1, tm, d), lambda b, i, j: (b, i, 0))
    return pl.pallas_call(
        functools.partial(_ffn_kernel, final=final),
        grid=(bsz, s // tm, dff // tf),
        in_specs=[row,
                  pl.BlockSpec((1, 6, d), lambda b, i, j: (b, 0, 0)),
                  pl.BlockSpec((1, d), lambda b, i, j: (0, 0)),
                  pl.BlockSpec((d, tf), lambda b, i, j: (0, j)),
                  pl.BlockSpec((d, tf), lambda b, i, j: (0, j)),
                  pl.BlockSpec((tf, d), lambda b, i, j: (j, 0)),
                  pl.BlockSpec((1, d), lambda b, i, j: (0, 0)),
                  pl.BlockSpec((1, 2, d), lambda b, i, j: (b, 0, 0))],
        out_specs=row,
        out_shape=jax.ShapeDtypeStruct((bsz, s, d), F32),
        scratch_shapes=[pltpu.VMEM((tm, d), BF16), pltpu.VMEM((tm, d), F32)],
        compiler_params=pltpu.CompilerParams(
            dimension_semantics=("parallel", "parallel", "arbitrary"),
            vmem_limit_bytes=VMEM_LIMIT),
        name="swiglu_ffn",
    )(x, mod, gain, wg, wu, wd, fgain, fmod)


def kernel(x, c, w_mod, b_mod, norm1_gain, norm2_gain, w_in, kv_gain, w_uk, w_uv, sb_out_gain,
           dsa_out_gain, w_o, w_gate, w_up, w_down, w_mod_final, b_mod_final, final_gain):
    bsz, s, d = x.shape
    depth = w_in.shape[0]
    sb_w = sb_out_gain.shape[1]
    dsa_w = dsa_out_gain.shape[1]
    kv_w = kv_gain.shape[1]
    topk = min(TOPK_MAX, s // 4)
    assert s % (2 * LANES) == 0 and HEAD_DIM * 2 == LANES and w_uk.shape[2] == HEAD_DIM

    mod = _mod(c, w_mod, b_mod).reshape(depth, bsz, 6, d)
    fmod = _mod(c, w_mod_final[None], b_mod_final[None]).reshape(bsz, 2, d)

    n_a = 3 * sb_w + dsa_w + kv_w
    n_idx = N_IDX_HEADS * IDX_DIM + IDX_DIM + N_IDX_HEADS
    idx_pad = N_IDX_HEADS * IDX_DIM + LANES - n_idx
    ones = jnp.ones((HEAD_DIM,), F32)
    zeros = jnp.zeros((HEAD_DIM,), F32)
    bkv = jnp.concatenate([zeros, zeros, zeros, zeros, ones, zeros, zeros, zeros, zeros, ones])[None]

    for l in range(depth):
        wa = w_in[l][:, :n_a].astype(BF16)
        wi = jnp.pad(w_in[l][:, n_a:], ((0, 0), (0, idx_pad)))
        wih = wi.astype(BF16)
        wil = (wi - wih.astype(F32)).astype(BF16)
        zk = jnp.zeros_like(w_uk[l])
        wkv = jnp.concatenate([w_uk[l], w_uk[l], w_uv[l], zk, zk, zk, zk, w_uv[l], zk, zk],
                              axis=1).astype(BF16)
        (sbq, sbk, sbv, dq, kd2, vde, vdo, qih, qil, ki, wt) = _in_proj(
            x, mod[l], norm1_gain[l][None], wa, wih, wil, kv_gain[l][None], wkv, bkv,
            sb_w=sb_w, dsa_w=dsa_w, kv_w=kv_w)
        osb = _sb_attention(sbq, sbk, sbv)
        odsa = _dsa_attention(dq, kd2, vde, vdo, qih, qil, ki, wt, topk=topk)
        x = _out_proj(x, osb, odsa, mod[l], sb_out_gain[l][None], dsa_out_gain[l][None],
                      w_o[l].astype(BF16))
        x = _ffn(x, mod[l], norm2_gain[l][None], w_gate[l].astype(BF16), w_up[l].astype(BF16),
                 w_down[l].astype(BF16), final_gain[None], fmod, final=(l == depth - 1))
    return x
```

```python
import functools

import jax
import jax.numpy as jnp
from jax import lax
from jax.experimental import pallas as pl
from jax.experimental.pallas import tpu as pltpu

F32 = jnp.float32
BF16 = jnp.bfloat16
I32 = jnp.int32

HEAD_DIM = 64
LANES = 128
Q_BLOCK = 128
N_IDX_HEADS = 4
IDX_DIM = 64
TOPK_MAX = 256
RMS_EPS = 1e-6
INT_MIN = -2 ** 31
NEG_BIG = -1e30
VMEM_LIMIT = 52 * 1024 * 1024
SB_KC = 512
DSA_KC = 512


def _nt(a, b):
    return lax.dot_general(a, b, (((1,), (1,)), ((), ())), preferred_element_type=F32)


def _dot(a, b):
    return jnp.dot(a, b, preferred_element_type=F32)


def _split(x):
    hi = x.astype(BF16)
    lo = (x - hi.astype(F32)).astype(BF16)
    return hi, lo


def _rms(x, gain):
    ms = jnp.mean(x * x, axis=-1, keepdims=True)
    return x * lax.rsqrt(ms + RMS_EPS) * gain


def _mod_kernel(c_ref, w_ref, b_ref, o_ref):
    c = c_ref[...]
    ca = c / (1.0 + jnp.exp(-c))
    ch, cl = _split(ca)
    wh, wl = _split(w_ref[0])
    o_ref[0] = _dot(ch, wh) + _dot(ch, wl) + _dot(cl, wh) + b_ref[0]


def _mod(c, w, b, tn=1024):
    nl, d, n = w.shape
    bsz = c.shape[0]
    return pl.pallas_call(
        _mod_kernel,
        grid=(nl, n // tn),
        in_specs=[pl.BlockSpec((bsz, d), lambda l, j: (0, 0)),
                  pl.BlockSpec((1, d, tn), lambda l, j: (l, 0, j)),
                  pl.BlockSpec((1, 1, tn), lambda l, j: (l, 0, j))],
        out_specs=pl.BlockSpec((1, bsz, tn), lambda l, j: (l, 0, j)),
        out_shape=jax.ShapeDtypeStruct((nl, bsz, n), F32),
        name="adaln_mod",
    )(c, w, b.reshape(nl, 1, n))


def _in_kernel(x_ref, mod_ref, g_ref, wa_ref, wih_ref, wil_ref, kvg_ref, wkv_ref, bkv_ref,
               sbq_ref, sbk_ref, sbv_ref, dq_ref, kda_ref, qi2_ref, ki1_ref, wt_ref, vd1t_ref,
               *, sb_w, dsa_w, kv_w):
    tm = x_ref.shape[1]
    x = x_ref[0]
    mod = mod_ref[0]
    h = _rms(x, g_ref[...]) * (1.0 + mod[1:2]) + mod[0:1]
    hh, hl = _split(h)
    main = _dot(hh, wa_ref[...])
    scale = HEAD_DIM ** -0.5
    sbq_ref[0] = (main[:, 0:sb_w] * scale).astype(BF16)
    sbk_ref[0] = main[:, sb_w:2 * sb_w].astype(BF16)
    sbv_ref[0] = main[:, 2 * sb_w:3 * sb_w].astype(BF16)
    o = 3 * sb_w
    dq_ref[0] = (main[:, o:o + dsa_w] * scale).astype(BF16)
    o += dsa_w
    kv = _rms(main[:, o:o + kv_w], kvg_ref[...])
    kvp = _dot(kv.astype(BF16), wkv_ref[...]) + bkv_ref[...]
    lane = lax.broadcasted_iota(I32, (tm, LANES), 1)
    pos = pl.program_id(1) * tm + lax.broadcasted_iota(I32, (tm, LANES), 0)
    posf = jnp.where(lane == 0, pos >> 7, jnp.where(lane == 1, pos & (LANES - 1), 0)).astype(F32)
    kda_ref[0, :, 0:LANES] = kvp[:, 0:LANES].astype(BF16)
    kda_ref[0, :, LANES:2 * LANES] = posf.astype(BF16)
    vd1t_ref[0] = kvp[:, LANES:2 * LANES].T.astype(BF16)
    wih = wih_ref[...]
    idx = _dot(hh, wih) + _dot(hh, wil_ref[...]) + _dot(hl, wih)
    first = lane < IDX_DIM
    for pr in range(N_IDX_HEADS // 2):
        pair = idx[:, pr * LANES:(pr + 1) * LANES] * (IDX_DIM ** -0.5)
        rolled = pltpu.roll(pair, IDX_DIM, 1)
        for half in range(2):
            dup = jnp.where(first, pair, rolled) if half == 0 else jnp.where(first, rolled, pair)
            qh, ql = _split(dup)
            hd = 2 * pr + half
            qi2_ref[0, :, 2 * hd * LANES:(2 * hd + 1) * LANES] = qh
            qi2_ref[0, :, (2 * hd + 1) * LANES:(2 * hd + 2) * LANES] = jnp.where(
                first, ql, jnp.zeros_like(ql))
    nq = N_IDX_HEADS * IDX_DIM
    kw = idx[:, nq:nq + LANES]
    kh, kl = _split(jnp.where(first, kw, pltpu.roll(kw, IDX_DIM, 1)))
    ki1_ref[0] = jnp.where(first, kh, kl)
    kwt = kw.T
    wt_ref[0] = kwt[IDX_DIM:IDX_DIM + 8, :] * (N_IDX_HEADS ** -0.5)


def _in_proj(x, mod, gain, wa, wih, wil, kvg, wkv, bkv, *, sb_w, dsa_w, kv_w, tm=512):
    bsz, s, d = x.shape
    na = wa.shape[1]
    ni = wih.shape[1]
    nkv = wkv.shape[1]
    full = lambda shape: pl.BlockSpec(shape, lambda b, i: (0,) * len(shape))
    row = lambda w: pl.BlockSpec((1, tm, w), lambda b, i: (b, i, 0))
    outs = [
        (sb_w, BF16), (sb_w, BF16), (sb_w, BF16), (dsa_w, BF16),
        (2 * LANES, BF16), (2 * LANES * N_IDX_HEADS, BF16), (LANES, BF16),
    ]
    out_shape = [jax.ShapeDtypeStruct((bsz, s, w), dt) for w, dt in outs]
    out_specs = [row(w) for w, _ in outs]
    for rows, dt in ((8, F32), (LANES, BF16)):
        out_shape.append(jax.ShapeDtypeStruct((bsz, rows, s), dt))
        out_specs.append(pl.BlockSpec((1, rows, tm), lambda b, i: (b, 0, i)))
    return pl.pallas_call(
        functools.partial(_in_kernel, sb_w=sb_w, dsa_w=dsa_w, kv_w=kv_w),
        grid=(bsz, s // tm),
        in_specs=[row(d),
                  pl.BlockSpec((1, 6, d), lambda b, i: (b, 0, 0)),
                  full((1, d)), full((d, na)), full((d, ni)), full((d, ni)),
                  full((1, kv_w)), full((kv_w, nkv)), full((1, nkv))],
        out_specs=out_specs,
        out_shape=out_shape,
        compiler_params=pltpu.CompilerParams(
            dimension_semantics=("parallel", "parallel"), vmem_limit_bytes=VMEM_LIMIT),
        name="in_proj",
    )(x, mod, gain, wa, wih, wil, kvg, wkv, bkv)


def _sb_kernel(q_ref, k_ref, v_ref, o_ref, car_ref, *, npair):
    i = pl.program_id(1)
    t = Q_BLOCK
    kc = SB_KC
    nsub = kc // LANES
    lane = lax.broadcasted_iota(I32, (t, LANES), 1)
    first = lane < HEAD_DIM
    zb = jnp.zeros((t, LANES), BF16)
    r2 = lax.broadcasted_iota(I32, (2 * LANES, 2 * LANES), 0) & (LANES - 1)
    c2 = lax.broadcasted_iota(I32, (2 * LANES, 2 * LANES), 1)
    uo2 = jnp.where((c2 >= LANES) | (r2 > c2), 1.0, 0.0).astype(BF16)
    firstk = lax.broadcasted_iota(I32, (kc, LANES), 1) < HEAD_DIM
    zk = jnp.zeros((kc, LANES), BF16)
    ct = i // nsub

    def pair_chunk(pr, c, top):
        cols = slice(pr * LANES, (pr + 1) * LANES)
        start = pl.multiple_of(c * kc, kc)
        q2 = q_ref[0, :, cols]
        qs = jnp.concatenate([jnp.where(first, q2, zb), jnp.where(first, zb, q2)], axis=0)
        k2 = k_ref[0, pl.ds(start, kc), cols]
        v2 = v_ref[0, pl.ds(start, kc), cols]
        z = _nt(qs, k2)
        u = jnp.concatenate([z[:, j * LANES:(j + 1) * LANES] for j in range(nsub)], axis=0)
        nabs = lax.bitcast_convert_type(lax.bitcast_convert_type(u, I32) | INT_MIN, F32)
        sp = jnp.maximum(u, 0.0) + jnp.log(1.0 + jnp.exp(nabs))
        d = u - sp
        if top:
            rr = lax.broadcasted_iota(I32, (nsub * 2 * t, LANES), 0)
            ll = lax.broadcasted_iota(I32, (nsub * 2 * t, LANES), 1)
            rel = (rr >> ((2 * t).bit_length() - 1)) * LANES + ll - (rr & (t - 1))
            off = i * t - c * kc
            sp = jnp.where(rel < off, sp, 0.0)
        hi, lo = _split(sp)
        cs = _dot(jnp.concatenate([hi, lo], axis=1), uo2)
        run = jnp.zeros((2 * t, LANES), F32) if top else car_ref[pr]
        pieces = [None] * nsub
        for j in reversed(range(nsub)):
            lo_r, hi_r = j * 2 * t, (j + 1) * 2 * t
            a = jnp.exp(d[lo_r:hi_r] - (cs[lo_r:hi_r, :LANES] + run))
            if top:
                a = jnp.where(rel[lo_r:hi_r] < off, a, 0.0)
            pieces[j] = a.astype(BF16)
            run = run + cs[lo_r:hi_r, LANES:]
        car_ref[pr] = run
        a_e = jnp.concatenate([p[:t] for p in pieces], axis=1)
        a_o = jnp.concatenate([p[t:] for p in pieces], axis=1)
        pv = _dot(a_e, jnp.where(firstk, v2, zk)) + _dot(a_o, jnp.where(firstk, zk, v2))
        if top:
            o_ref[0, :, cols] = pv
        else:
            o_ref[0, :, cols] += pv

    for pr in range(npair):
        pair_chunk(pr, ct, True)

    def body(j, carry):
        for pr in range(npair):
            pair_chunk(pr, ct - 1 - j, False)
        return carry

    lax.fori_loop(0, ct, body, 0)


def _sb_attention(q, k, v):
    bsz, s, w = q.shape
    npair = w // LANES
    nq = s // Q_BLOCK
    return pl.pallas_call(
        functools.partial(_sb_kernel, npair=npair),
        grid=(bsz, nq),
        in_specs=[pl.BlockSpec((1, Q_BLOCK, w), lambda b, i: (b, i, 0)),
                  pl.BlockSpec((1, s, w), lambda b, i: (b, 0, 0)),
                  pl.BlockSpec((1, s, w), lambda b, i: (b, 0, 0))],
        out_specs=pl.BlockSpec((1, Q_BLOCK, w), lambda b, i: (b, i, 0)),
        out_shape=jax.ShapeDtypeStruct((bsz, s, w), F32),
        scratch_shapes=[pltpu.VMEM((npair, 2 * Q_BLOCK, LANES), F32)],
        compiler_params=pltpu.CompilerParams(
            dimension_semantics=("parallel", "arbitrary"), vmem_limit_bytes=VMEM_LIMIT),
        name="sb_attention",
    )(q, k, v)


def _sortable(x):
    b = lax.bitcast_convert_type(x + 0.0, I32)
    return b ^ ((b >> 31) & 0x7FFFFFFF)


def _dsa_kernel(dq_ref, kda_ref, vd1t_ref, qi2_ref, ki1_ref, wt_ref, o_ref,
                key_ref, qa_ref, acc_ref, m_ref, *, topk, n_heads):
    i = pl.program_id(1)
    t = Q_BLOCK
    kc = DSA_KC
    n_kc = i // (kc // t) + 1
    n128 = n_kc * (kc // t)
    kf = float(topk)
    lane = lax.broadcasted_iota(I32, (t, LANES), 1)
    row = lax.broadcasted_iota(I32, (t, LANES), 0)
    first = lane < HEAD_DIM

    wt = wt_ref[0]
    rowlane = (lax.broadcasted_iota(I32, (kc, LANES), 0)
               - lax.broadcasted_iota(I32, (kc, LANES), 1))

    def score_chunk(c, carry):
        start = pl.multiple_of(c * kc, kc)
        k1 = ki1_ref[0, pl.ds(start, kc), :]
        lhs = jnp.concatenate([k1, k1], axis=1)
        score = None
        for pr in range(N_IDX_HEADS // 2):
            w2 = jnp.concatenate(
                [qi2_ref[0, :, 2 * hd * LANES:(2 * hd + 2) * LANES] for hd in (2 * pr, 2 * pr + 1)],
                axis=0)
            lg = _nt(lhs, w2)
            for half in range(2):
                hd = 2 * pr + half
                term = jnp.maximum(lg[:, half * LANES:(half + 1) * LANES], 0.0) * wt[hd:hd + 1, :]
                score = term if score is None else score + term
        key = _sortable(score)
        key_ref[pl.ds(start, kc), :] = jnp.where(rowlane > i * t - c * kc, INT_MIN, key)
        return carry

    lax.fori_loop(0, n_kc, score_chunk, 0)

    def count(pred):
        def body(c, acc):
            start = pl.multiple_of(c * kc, kc)
            m = jnp.where(pred(key_ref[pl.ds(start, kc), :]), 1.0, 0.0)
            m = m.reshape(8, kc // 8, LANES).sum(axis=0)
            return acc + m.reshape(kc // 64, 8, LANES).sum(axis=0)
        part = lax.fori_loop(0, n_kc, body, jnp.zeros((8, LANES), F32))
        return jnp.sum(part, axis=0, keepdims=True)

    def bit_step(bi, lo):
        cand = lo + lax.shift_left(jnp.int32(1), 31 - bi)
        cnt = count(lambda blk: blk >= cand)
        return jnp.where(cnt >= kf, cand, lo)

    nbits = jnp.where((i + 1) * t > topk, 32, 0)
    thr = lax.fori_loop(0, nbits, bit_step, jnp.full((1, LANES), INT_MIN, I32))
    cnt_ge = count(lambda blk: blk >= thr)

    @pl.when((jnp.max(cnt_ge) > kf) & (nbits > 0))
    def _():
        need = kf - count(lambda blk: blk > thr)
        sl = jnp.where(lane < row, 1.0, 0.0).astype(BF16)

        def body(c, before):
            start = pl.multiple_of(c * t, t)
            blk = key_ref[pl.ds(start, t), :]
            eq = blk == thr
            eqf = jnp.where(eq, 1.0, 0.0)
            rank = _dot(sl, eqf.astype(BF16)) + before
            key_ref[pl.ds(start, t), :] = jnp.where(eq & (rank >= need), INT_MIN, blk)
            return before + jnp.sum(eqf, axis=0, keepdims=True)

        lax.fori_loop(0, n128, body, jnp.zeros((1, LANES), F32))

    thr_eff = jnp.maximum(thr, INT_MIN + 1)

    zb = jnp.zeros((t, LANES), BF16)
    for hd in range(n_heads):
        q2 = dq_ref[0, :, (hd // 2) * LANES:(hd // 2 + 1) * LANES]
        slope = 2.0 ** (-8.0 * (hd + 1) / n_heads)
        aug = jnp.where(lane == 0, LANES * slope, jnp.where(lane == 1, slope, 0.0)).astype(BF16)
        qa_ref[hd * t:(hd + 1) * t, 0:LANES] = (
            jnp.where(first, q2, zb) if hd % 2 == 0 else jnp.where(first, zb, q2))
        qa_ref[hd * t:(hd + 1) * t, LANES:2 * LANES] = aug
    m_ref[...] = jnp.full(m_ref.shape, NEG_BIG, F32)
    acc_ref[...] = jnp.zeros(acc_ref.shape, F32)

    def attn_chunk(c, carry):
        start = pl.multiple_of(c * kc, kc)
        kd = kda_ref[0, pl.ds(start, kc), :]
        vt = vd1t_ref[0, :, pl.ds(start, kc)]
        bias = jnp.where(key_ref[pl.ds(start, kc), :] >= thr_eff, 0.0, -jnp.inf)
        sc = _nt(kd, qa_ref[...]) + jnp.concatenate([bias] * n_heads, axis=1)
        m_old = m_ref[...]
        m_new = jnp.maximum(m_old, jnp.max(sc, axis=0, keepdims=True))
        p = jnp.exp(sc - m_new)
        acc_ref[...] = acc_ref[...] * jnp.exp(m_old - m_new) + _dot(vt, p.astype(BF16))
        m_ref[...] = m_new
        return carry

    lax.fori_loop(0, n_kc, attn_chunk, 0)

    for pr in range(n_heads // 2):
        a_e = acc_ref[:, (2 * pr) * t:(2 * pr + 1) * t]
        a_o = acc_ref[:, (2 * pr + 1) * t:(2 * pr + 2) * t]
        r = jnp.concatenate([a_e[:HEAD_DIM] / a_e[HEAD_DIM:], a_o[:HEAD_DIM] / a_o[HEAD_DIM:]], axis=0)
        o_ref[0, :, pr * LANES:(pr + 1) * LANES] = r.T


def _dsa_attention(dq, kda, vd1t, qi2, ki1, wt, *, topk):
    bsz, s, w = dq.shape
    nq = s // Q_BLOCK
    n_heads = w // HEAD_DIM
    blk = lambda wd: pl.BlockSpec((1, Q_BLOCK, wd), lambda b, i: (b, i, 0))
    full = lambda wd: pl.BlockSpec((1, s, wd), lambda b, i: (b, 0, 0))
    return pl.pallas_call(
        functools.partial(_dsa_kernel, topk=topk, n_heads=n_heads),
        grid=(bsz, nq),
        in_specs=[blk(w), full(2 * LANES),
                  pl.BlockSpec((1, LANES, s), lambda b, i: (b, 0, 0)),
                  blk(qi2.shape[2]), full(LANES),
                  pl.BlockSpec((1, 8, Q_BLOCK), lambda b, i: (b, 0, i))],
        out_specs=blk(w),
        out_shape=jax.ShapeDtypeStruct((bsz, s, w), F32),
        scratch_shapes=[pltpu.VMEM((s, LANES), I32),
                        pltpu.VMEM((n_heads * Q_BLOCK, 2 * LANES), BF16),
                        pltpu.VMEM((LANES, n_heads * Q_BLOCK), F32),
                        pltpu.VMEM((1, n_heads * Q_BLOCK), F32)],
        compiler_params=pltpu.CompilerParams(
            dimension_semantics=("parallel", "arbitrary"), vmem_limit_bytes=VMEM_LIMIT),
        name="dsa_attention",
    )(dq, kda, vd1t, qi2, ki1, wt)


def _out_kernel(x_ref, osb_ref, odsa_ref, mod_ref, gsb_ref, gdsa_ref, wo_ref, o_ref, *, sb_w):
    a = _rms(osb_ref[0], gsb_ref[...]).astype(BF16)
    b = _rms(odsa_ref[0], gdsa_ref[...]).astype(BF16)
    y = _dot(a, wo_ref[0:sb_w, :]) + _dot(b, wo_ref[sb_w:, :])
    o_ref[0] = x_ref[0] + mod_ref[0][2:3] * y


def _out_proj(x, osb, odsa, mod, gsb, gdsa, wo, tm=512):
    bsz, s, d = x.shape
    sb_w, dsa_w = osb.shape[2], odsa.shape[2]
    row = lambda w: pl.BlockSpec((1, tm, w), lambda b, i: (b, i, 0))
    full = lambda shape: pl.BlockSpec(shape, lambda b, i: (0,) * len(shape))
    return pl.pallas_call(
        functools.partial(_out_kernel, sb_w=sb_w),
        grid=(bsz, s // tm),
        in_specs=[row(d), row(sb_w), row(dsa_w),
                  pl.BlockSpec((1, 6, d), lambda b, i: (b, 0, 0)),
                  full((1, sb_w)), full((1, dsa_w)), full((sb_w + dsa_w, d))],
        out_specs=row(d),
        out_shape=jax.ShapeDtypeStruct((bsz, s, d), F32),
        compiler_params=pltpu.CompilerParams(
            dimension_semantics=("parallel", "parallel"), vmem_limit_bytes=VMEM_LIMIT),
        name="out_proj",
    )(x, osb, odsa, mod, gsb, gdsa, wo)


def _ffn_kernel(x_ref, mod_ref, g_ref, wg_ref, wu_ref, wd_ref, fg_ref, fmod_ref, o_ref,
                h_ref, acc_ref, *, final):
    j = pl.program_id(2)

    @pl.when(j == 0)
    def _():
        mod = mod_ref[0]
        h = _rms(x_ref[0], g_ref[...]) * (1.0 + mod[4:5]) + mod[3:4]
        h_ref[...] = h.astype(BF16)

    h = h_ref[...]
    g = _dot(h, wg_ref[...])
    u = _dot(h, wu_ref[...])
    act = (g / (1.0 + jnp.exp(-g))) * u
    part = _dot(act.astype(BF16), wd_ref[...])

    @pl.when(j == 0)
    def _():
        acc_ref[...] = part

    @pl.when(j > 0)
    def _():
        acc_ref[...] += part

    @pl.when(j == pl.num_programs(2) - 1)
    def _():
        y = x_ref[0] + mod_ref[0][5:6] * acc_ref[...]
        if final:
            fmod = fmod_ref[0]
            y = _rms(y, fg_ref[...]) * (1.0 + fmod[1:2]) + fmod[0:1]
        o_ref[0] = y


def _ffn(x, mod, gain, wg, wu, wd, fgain, fmod, *, final, tm=1024, tf=256):
    bsz, s, d = x.shape
    dff = wg.shape[1]
    tm = min(tm, s)
    row = pl.BlockSpec((1, tm, d), lambda b, i, j: (b, i, 0))
    return pl.pallas_call(
        functools.partial(_ffn_kernel, final=final),
        grid=(bsz, s // tm, dff // tf),
        in_specs=[row,
                  pl.BlockSpec((1, 6, d), lambda b, i, j: (b, 0, 0)),
                  pl.BlockSpec((1, d), lambda b, i, j: (0, 0)),
                  pl.BlockSpec((d, tf), lambda b, i, j: (0, j)),
                  pl.BlockSpec((d, tf), lambda b, i, j: (0, j)),
                  pl.BlockSpec((tf, d), lambda b, i, j: (j, 0)),
                  pl.BlockSpec((1, d), lambda b, i, j: (0, 0)),
                  pl.BlockSpec((1, 2, d), lambda b, i, j: (b, 0, 0))],
        out_specs=row,
        out_shape=jax.ShapeDtypeStruct((bsz, s, d), F32),
        scratch_shapes=[pltpu.VMEM((tm, d), BF16), pltpu.VMEM((tm, d), F32)],
        compiler_params=pltpu.CompilerParams(
            dimension_semantics=("parallel", "parallel", "arbitrary"),
            vmem_limit_bytes=VMEM_LIMIT),
        name="swiglu_ffn",
    )(x, mod, gain, wg, wu, wd, fgain, fmod)


def kernel(x, c, w_mod, b_mod, norm1_gain, norm2_gain, w_in, kv_gain, w_uk, w_uv, sb_out_gain,
           dsa_out_gain, w_o, w_gate, w_up, w_down, w_mod_final, b_mod_final, final_gain):
    bsz, s, d = x.shape
    depth = w_in.shape[0]
    sb_w = sb_out_gain.shape[1]
    dsa_w = dsa_out_gain.shape[1]
    kv_w = kv_gain.shape[1]
    topk = min(TOPK_MAX, s // 4)
    assert s % max(SB_KC, DSA_KC) == 0 and topk % Q_BLOCK == 0
    assert HEAD_DIM * 2 == LANES and w_uk.shape[2] == HEAD_DIM and s // LANES <= 256

    mod = _mod(c, w_mod, b_mod).reshape(depth, bsz, 6, d)
    fmod = _mod(c, w_mod_final[None], b_mod_final[None]).reshape(bsz, 2, d)

    n_a = 3 * sb_w + dsa_w + kv_w
    n_idx = N_IDX_HEADS * IDX_DIM + IDX_DIM + N_IDX_HEADS
    idx_pad = N_IDX_HEADS * IDX_DIM + LANES - n_idx
    bkv = jnp.concatenate([jnp.zeros((3 * HEAD_DIM,), F32), jnp.ones((HEAD_DIM,), F32)])[None]

    for l in range(depth):
        wa = w_in[l][:, :n_a].astype(BF16)
        wi = jnp.pad(w_in[l][:, n_a:], ((0, 0), (0, idx_pad)))
        wih = wi.astype(BF16)
        wil = (wi - wih.astype(F32)).astype(BF16)
        wkv = jnp.concatenate([w_uk[l], w_uk[l], w_uv[l], jnp.zeros_like(w_uv[l])],
                              axis=1).astype(BF16)
        (sbq, sbk, sbv, dq, kda, qi2, ki1, wt, vd1t) = _in_proj(
            x, mod[l], norm1_gain[l][None], wa, wih, wil, kv_gain[l][None], wkv, bkv,
            sb_w=sb_w, dsa_w=dsa_w, kv_w=kv_w)
        osb = _sb_attention(sbq, sbk, sbv)
        odsa = _dsa_attention(dq, kda, vd1t, qi2, ki1, wt, topk=topk)
        x = _out_proj(x, osb, odsa, mod[l], sb_out_gain[l][None], dsa_out_gain[l][None],
                      w_o[l].astype(BF16))
        x = _ffn(x, mod[l], norm2_gain[l][None], w_gate[l].astype(BF16), w_up[l].astype(BF16),
                 w_down[l].astype(BF16), final_gain[None], fmod, final=(l == depth - 1))
    return x
```

```python
import functools

import jax
import jax.numpy as jnp
from jax import lax
from jax.experimental import pallas as pl
from jax.experimental.pallas import tpu as pltpu

F32 = jnp.float32
BF16 = jnp.bfloat16
I32 = jnp.int32

HEAD_DIM = 64
LANES = 128
Q_BLOCK = 128
N_IDX_HEADS = 4
IDX_DIM = 64
TOPK_MAX = 256
RMS_EPS = 1e-6
INT_MIN = -2 ** 31
NEG_BIG = -1e30
VMEM_LIMIT = 52 * 1024 * 1024
SB_KC = 512
SB_ROW_TILE = 64
DSA_KC = 512


def _nt(a, b):
    return lax.dot_general(a, b, (((1,), (1,)), ((), ())), preferred_element_type=F32)


def _dot(a, b):
    return jnp.dot(a, b, preferred_element_type=F32)


def _split(x):
    hi = x.astype(BF16)
    lo = (x - hi.astype(F32)).astype(BF16)
    return hi, lo


def _rms(x, gain):
    ms = jnp.mean(x * x, axis=-1, keepdims=True)
    return x * lax.rsqrt(ms + RMS_EPS) * gain


def _mod_kernel(c_ref, w_ref, b_ref, o_ref):
    c = c_ref[...]
    ca = c / (1.0 + jnp.exp(-c))
    ch, cl = _split(ca)
    wh, wl = _split(w_ref[0])
    o_ref[0] = _dot(ch, wh) + _dot(ch, wl) + _dot(cl, wh) + b_ref[0]


def _mod(c, w, b, tn=1024):
    nl, d, n = w.shape
    bsz = c.shape[0]
    return pl.pallas_call(
        _mod_kernel,
        grid=(nl, n // tn),
        in_specs=[pl.BlockSpec((bsz, d), lambda l, j: (0, 0)),
                  pl.BlockSpec((1, d, tn), lambda l, j: (l, 0, j)),
                  pl.BlockSpec((1, 1, tn), lambda l, j: (l, 0, j))],
        out_specs=pl.BlockSpec((1, bsz, tn), lambda l, j: (l, 0, j)),
        out_shape=jax.ShapeDtypeStruct((nl, bsz, n), F32),
        name="adaln_mod",
    )(c, w, b.reshape(nl, 1, n))


def _in_kernel(x_ref, mod_ref, g_ref, wa_ref, wih_ref, wil_ref, kvg_ref, wkv_ref, bkv_ref,
               sbq_ref, sbk_ref, sbv_ref, dq_ref, kda_ref, qi2_ref, ki1_ref, wt_ref, vd1t_ref,
               *, sb_w, dsa_w, kv_w):
    tm = x_ref.shape[1]
    x = x_ref[0]
    mod = mod_ref[0]
    h = _rms(x, g_ref[...]) * (1.0 + mod[1:2]) + mod[0:1]
    hh, hl = _split(h)
    main = _dot(hh, wa_ref[...])
    scale = HEAD_DIM ** -0.5
    sbq_ref[0] = (main[:, 0:sb_w] * scale).astype(BF16)
    sbk_ref[0] = main[:, sb_w:2 * sb_w].astype(BF16)
    sbv_ref[0] = main[:, 2 * sb_w:3 * sb_w].astype(BF16)
    o = 3 * sb_w
    dq_ref[0] = (main[:, o:o + dsa_w] * scale).astype(BF16)
    o += dsa_w
    kv = _rms(main[:, o:o + kv_w], kvg_ref[...])
    kvp = _dot(kv.astype(BF16), wkv_ref[...]) + bkv_ref[...]
    lane = lax.broadcasted_iota(I32, (tm, LANES), 1)
    pos = pl.program_id(1) * tm + lax.broadcasted_iota(I32, (tm, LANES), 0)
    posf = jnp.where(lane == 0, pos >> 7, jnp.where(lane == 1, pos & (LANES - 1), 0)).astype(F32)
    kda_ref[0, :, 0:LANES] = kvp[:, 0:LANES].astype(BF16)
    kda_ref[0, :, LANES:2 * LANES] = posf.astype(BF16)
    vd1t_ref[0] = kvp[:, LANES:2 * LANES].T.astype(BF16)
    wih = wih_ref[...]
    idx = _dot(hh, wih) + _dot(hh, wil_ref[...]) + _dot(hl, wih)
    first = lane < IDX_DIM
    for pr in range(N_IDX_HEADS // 2):
        pair = idx[:, pr * LANES:(pr + 1) * LANES] * (IDX_DIM ** -0.5)
        rolled = pltpu.roll(pair, IDX_DIM, 1)
        for half in range(2):
            dup = jnp.where(first, pair, rolled) if half == 0 else jnp.where(first, rolled, pair)
            qh, ql = _split(dup)
            hd = 2 * pr + half
            qi2_ref[0, :, 2 * hd * LANES:(2 * hd + 1) * LANES] = qh
            qi2_ref[0, :, (2 * hd + 1) * LANES:(2 * hd + 2) * LANES] = jnp.where(
                first, ql, jnp.zeros_like(ql))
    nq = N_IDX_HEADS * IDX_DIM
    kw = idx[:, nq:nq + LANES]
    kh, kl = _split(jnp.where(first, kw, pltpu.roll(kw, IDX_DIM, 1)))
    ki1_ref[0] = jnp.where(first, kh, kl)
    kwt = kw.T
    wt_ref[0] = kwt[IDX_DIM:IDX_DIM + 8, :] * (N_IDX_HEADS ** -0.5)


def _in_proj(x, mod, gain, wa, wih, wil, kvg, wkv, bkv, *, sb_w, dsa_w, kv_w, tm=512):
    bsz, s, d = x.shape
    na = wa.shape[1]
    ni = wih.shape[1]
    nkv = wkv.shape[1]
    full = lambda shape: pl.BlockSpec(shape, lambda b, i: (0,) * len(shape))
    row = lambda w: pl.BlockSpec((1, tm, w), lambda b, i: (b, i, 0))
    outs = [
        (sb_w, BF16), (sb_w, BF16), (sb_w, BF16), (dsa_w, BF16),
        (2 * LANES, BF16), (2 * LANES * N_IDX_HEADS, BF16), (LANES, BF16),
    ]
    out_shape = [jax.ShapeDtypeStruct((bsz, s, w), dt) for w, dt in outs]
    out_specs = [row(w) for w, _ in outs]
    for rows, dt in ((8, F32), (LANES, BF16)):
        out_shape.append(jax.ShapeDtypeStruct((bsz, rows, s), dt))
        out_specs.append(pl.BlockSpec((1, rows, tm), lambda b, i: (b, 0, i)))
    return pl.pallas_call(
        functools.partial(_in_kernel, sb_w=sb_w, dsa_w=dsa_w, kv_w=kv_w),
        grid=(bsz, s // tm),
        in_specs=[row(d),
                  pl.BlockSpec((1, 6, d), lambda b, i: (b, 0, 0)),
                  full((1, d)), full((d, na)), full((d, ni)), full((d, ni)),
                  full((1, kv_w)), full((kv_w, nkv)), full((1, nkv))],
        out_specs=out_specs,
        out_shape=out_shape,
        compiler_params=pltpu.CompilerParams(
            dimension_semantics=("parallel", "parallel"), vmem_limit_bytes=VMEM_LIMIT),
        name="in_proj",
    )(x, mod, gain, wa, wih, wil, kvg, wkv, bkv)


def _sb_kernel(q_ref, k_ref, v_ref, o_ref, car_ref, *, npair):
    i = pl.program_id(1)
    t = Q_BLOCK
    kc = SB_KC
    nsub = kc // LANES
    lane = lax.broadcasted_iota(I32, (t, LANES), 1)
    first = lane < HEAD_DIM
    zb = jnp.zeros((t, LANES), BF16)
    r2 = lax.broadcasted_iota(I32, (LANES, 2 * LANES), 0)
    c2 = lax.broadcasted_iota(I32, (LANES, 2 * LANES), 1)
    uo = jnp.where((c2 >= LANES) | (r2 > c2), 1.0, 0.0).astype(BF16)
    firstk = lax.broadcasted_iota(I32, (kc, LANES), 1) < HEAD_DIM
    zk = jnp.zeros((kc, LANES), BF16)
    ct = i // nsub

    def stage_scores(pr, c):
        cols = slice(pr * LANES, (pr + 1) * LANES)
        start = pl.multiple_of(c * kc, kc)
        q2 = q_ref[0, :, cols]
        qs = jnp.concatenate([jnp.where(first, q2, zb), jnp.where(first, zb, q2)], axis=0)
        z = _nt(qs, k_ref[0, pl.ds(start, kc), cols])
        return jnp.concatenate([z[:, j * LANES:(j + 1) * LANES] for j in range(nsub)], axis=0)

    rt = SB_ROW_TILE

    def causal_tile(r0, c):
        rr = r0 + lax.broadcasted_iota(I32, (rt, LANES), 0)
        ll = lax.broadcasted_iota(I32, (rt, LANES), 1)
        rel = (rr >> ((2 * t).bit_length() - 1)) * LANES + ll - (rr & (t - 1))
        return rel < i * t - c * kc

    def stage_cumsum(u, c, top):
        ds, sps = [], []
        for r0 in range(0, nsub * 2 * t, rt):
            ut = u[r0:r0 + rt]
            nabs = lax.bitcast_convert_type(lax.bitcast_convert_type(ut, I32) | INT_MIN, F32)
            sp = jnp.maximum(ut, 0.0) + jnp.log(1.0 + jnp.exp(nabs))
            ds.append(ut - sp)
            if top:
                sp = jnp.where(causal_tile(r0, c), sp, 0.0)
            sps.append(sp.astype(BF16))
        spb = jnp.concatenate(sps, axis=0)
        half = nsub * t
        cs = jnp.concatenate([_dot(spb[:half], uo), _dot(spb[half:], uo)], axis=0)
        return ds, cs

    def stage_output(pr, c, ds, cs, top):
        cols = slice(pr * LANES, (pr + 1) * LANES)
        start = pl.multiple_of(c * kc, kc)
        v2 = v_ref[0, pl.ds(start, kc), cols]
        ntile = 2 * t // rt
        runs = [jnp.zeros((rt, LANES), F32) if top else car_ref[pr, k * rt:(k + 1) * rt]
                for k in range(ntile)]
        pieces = [[None] * ntile for _ in range(nsub)]
        for j in reversed(range(nsub)):
            for k in range(ntile):
                r0 = j * 2 * t + k * rt
                a = jnp.exp(ds[r0 // rt] - (cs[r0:r0 + rt, :LANES] + runs[k]))
                if top:
                    a = jnp.where(causal_tile(r0, c), a, 0.0)
                pieces[j][k] = a.astype(BF16)
                runs[k] = runs[k] + cs[r0:r0 + rt, LANES:]
        for k in range(ntile):
            car_ref[pr, k * rt:(k + 1) * rt] = runs[k]
        pieces = [jnp.concatenate(p, axis=0) for p in pieces]
        a_e = jnp.concatenate([p[:t] for p in pieces], axis=1)
        a_o = jnp.concatenate([p[t:] for p in pieces], axis=1)
        pv = _dot(a_e, jnp.where(firstk, v2, zk)) + _dot(a_o, jnp.where(firstk, zk, v2))
        if top:
            o_ref[0, :, cols] = pv
        else:
            o_ref[0, :, cols] += pv

    def all_pairs(c, top):
        us, mids = {}, {}
        for step in range(npair + 2):
            if step < npair:
                us[step] = stage_scores(step, c)
            if 0 <= step - 1 < npair:
                mids[step - 1] = stage_cumsum(us.pop(step - 1), c, top)
            if 0 <= step - 2 < npair:
                stage_output(step - 2, c, *mids.pop(step - 2), top)

    all_pairs(ct, True)

    def body(j, carry):
        all_pairs(ct - 1 - j, False)
        return carry

    lax.fori_loop(0, ct, body, 0)


def _sb_attention(q, k, v):
    bsz, s, w = q.shape
    npair = w // LANES
    nq = s // Q_BLOCK
    return pl.pallas_call(
        functools.partial(_sb_kernel, npair=npair),
        grid=(bsz, nq),
        in_specs=[pl.BlockSpec((1, Q_BLOCK, w), lambda b, i: (b, i, 0)),
                  pl.BlockSpec((1, s, w), lambda b, i: (b, 0, 0)),
                  pl.BlockSpec((1, s, w), lambda b, i: (b, 0, 0))],
        out_specs=pl.BlockSpec((1, Q_BLOCK, w), lambda b, i: (b, i, 0)),
        out_shape=jax.ShapeDtypeStruct((bsz, s, w), F32),
        scratch_shapes=[pltpu.VMEM((npair, 2 * Q_BLOCK, LANES), F32)],
        compiler_params=pltpu.CompilerParams(
            dimension_semantics=("parallel", "arbitrary"), vmem_limit_bytes=VMEM_LIMIT),
        name="sb_attention",
    )(q, k, v)


def _sortable(x):
    b = lax.bitcast_convert_type(x + 0.0, I32)
    return b ^ ((b >> 31) & 0x7FFFFFFF)


def _dsa_kernel(dq_ref, kda_ref, vd1t_ref, qi2_ref, ki1_ref, wt_ref, o_ref,
                key_ref, qa_ref, acc_ref, m_ref, *, topk, n_heads):
    i = pl.program_id(1)
    t = Q_BLOCK
    kc = DSA_KC
    n_kc = i // (kc // t) + 1
    kf = float(topk)
    lane = lax.broadcasted_iota(I32, (t, LANES), 1)
    row = lax.broadcasted_iota(I32, (t, LANES), 0)
    first = lane < HEAD_DIM

    wt = wt_ref[0]
    rowlane = (lax.broadcasted_iota(I32, (kc, LANES), 0)
               - lax.broadcasted_iota(I32, (kc, LANES), 1))

    def score_chunk(c, carry):
        start = pl.multiple_of(c * kc, kc)
        k1 = ki1_ref[0, pl.ds(start, kc), :]
        lhs = jnp.concatenate([k1, k1], axis=1)
        score = None
        for pr in range(N_IDX_HEADS // 2):
            w2 = jnp.concatenate(
                [qi2_ref[0, :, 2 * hd * LANES:(2 * hd + 2) * LANES] for hd in (2 * pr, 2 * pr + 1)],
                axis=0)
            lg = _nt(lhs, w2)
            for half in range(2):
                hd = 2 * pr + half
                term = jnp.maximum(lg[:, half * LANES:(half + 1) * LANES], 0.0) * wt[hd:hd + 1, :]
                score = term if score is None else score + term
        key = _sortable(score)
        key_ref[pl.ds(start, kc), :] = jnp.where(rowlane > i * t - c * kc, INT_MIN, key)
        return carry

    lax.fori_loop(0, n_kc, score_chunk, 0)

    def count(pred):
        def body(c, acc):
            start = pl.multiple_of(c * kc, kc)
            m = jnp.where(pred(key_ref[pl.ds(start, kc), :]), 1.0, 0.0)
            m = m.reshape(8, kc // 8, LANES).sum(axis=0)
            return acc + m.reshape(kc // 64, 8, LANES).sum(axis=0)
        part = lax.fori_loop(0, n_kc, body, jnp.zeros((8, LANES), F32))
        return jnp.sum(part, axis=0, keepdims=True)

    searching = (i + 1) * t > topk

    def bit_step(bit, lo, done):
        cand = lo + lax.shift_left(jnp.int32(1), bit)
        cnt = count(lambda blk: blk >= cand)
        lo = jnp.where((done == 0.0) & (cnt >= kf), cand, lo)
        return lo, jnp.where(cnt == kf, 1.0, done)

    def search_cond(state):
        bit, _, _, active = state
        return (bit >= 0) & (active > 0.0)

    def search_body(state):
        bit, lo, done, _ = state
        active = jnp.max(1.0 - done)
        lo, done = bit_step(bit, lo, done)
        lo, done = bit_step(bit - 1, lo, done)
        return bit - 2, lo, done, active

    cnt0 = count(lambda blk: blk >= 0)
    cnt1 = count(lambda blk: blk >= 1)
    zero_tie = searching & (cnt0 >= kf) & (cnt1 < kf)
    lo0 = jnp.where(zero_tie, 0, jnp.full((1, LANES), INT_MIN, I32))
    done0 = jnp.where(searching, jnp.where(zero_tie, 1.0, 0.0), 1.0)
    state = (jnp.int32(31), lo0, done0, jnp.where(searching, 1.0, 0.0))
    _, thr, _, _ = lax.while_loop(search_cond, search_body, state)
    cnt_ge = count(lambda blk: blk >= thr)

    @pl.when((jnp.max(cnt_ge) > kf) & searching)
    def _():
        need = kf - count(lambda blk: blk > thr)
        sl = jnp.where(lane < row, 1.0, 0.0).astype(BF16)
        nsub = kc // t

        def body(c, before):
            start = pl.multiple_of(c * kc, kc)
            subs = [key_ref[pl.ds(start + j * t, t), :] for j in range(nsub)]
            eqfs = [jnp.where(sub == thr, 1.0, 0.0) for sub in subs]
            ranks = [_dot(sl, eqf.astype(BF16)) for eqf in eqfs]
            for j in range(nsub):
                demote = (subs[j] == thr) & (ranks[j] + before >= need)
                key_ref[pl.ds(start + j * t, t), :] = jnp.where(demote, INT_MIN, subs[j])
                before = before + jnp.sum(eqfs[j], axis=0, keepdims=True)
            return before

        lax.fori_loop(0, n_kc, body, jnp.zeros((1, LANES), F32))

    thr_eff = jnp.maximum(thr, INT_MIN + 1)

    zb = jnp.zeros((t, LANES), BF16)
    for hd in range(n_heads):
        q2 = dq_ref[0, :, (hd // 2) * LANES:(hd // 2 + 1) * LANES]
        slope = 2.0 ** (-8.0 * (hd + 1) / n_heads)
        aug = jnp.where(lane == 0, LANES * slope, jnp.where(lane == 1, slope, 0.0)).astype(BF16)
        qa_ref[hd * t:(hd + 1) * t, 0:LANES] = (
            jnp.where(first, q2, zb) if hd % 2 == 0 else jnp.where(first, zb, q2))
        qa_ref[hd * t:(hd + 1) * t, LANES:2 * LANES] = aug
    m_ref[...] = jnp.full(m_ref.shape, NEG_BIG, F32)
    acc_ref[...] = jnp.zeros(acc_ref.shape, F32)

    def attn_chunk(c, carry):
        start = pl.multiple_of(c * kc, kc)
        kd = kda_ref[0, pl.ds(start, kc), :]
        vt = vd1t_ref[0, :, pl.ds(start, kc)]
        bias = jnp.where(key_ref[pl.ds(start, kc), :] >= thr_eff, 0.0, -jnp.inf)
        sc = _nt(kd, qa_ref[...]) + jnp.concatenate([bias] * n_heads, axis=1)
        m_old = m_ref[...]
        m_new = jnp.maximum(m_old, jnp.max(sc, axis=0, keepdims=True))
        p = jnp.exp(sc - m_new)
        acc_ref[...] = acc_ref[...] * jnp.exp(m_old - m_new) + _dot(vt, p.astype(BF16))
        m_ref[...] = m_new
        return carry

    lax.fori_loop(0, n_kc, attn_chunk, 0)

    for pr in range(n_heads // 2):
        a_e = acc_ref[:, (2 * pr) * t:(2 * pr + 1) * t]
        a_o = acc_ref[:, (2 * pr + 1) * t:(2 * pr + 2) * t]
        r = jnp.concatenate([a_e[:HEAD_DIM] / a_e[HEAD_DIM:], a_o[:HEAD_DIM] / a_o[HEAD_DIM:]], axis=0)
        o_ref[0, :, pr * LANES:(pr + 1) * LANES] = r.T


def _dsa_attention(dq, kda, vd1t, qi2, ki1, wt, *, topk):
    bsz, s, w = dq.shape
    nq = s // Q_BLOCK
    n_heads = w // HEAD_DIM
    blk = lambda wd: pl.BlockSpec((1, Q_BLOCK, wd), lambda b, i: (b, i, 0))
    full = lambda wd: pl.BlockSpec((1, s, wd), lambda b, i: (b, 0, 0))
    return pl.pallas_call(
        functools.partial(_dsa_kernel, topk=topk, n_heads=n_heads),
        grid=(bsz, nq),
        in_specs=[blk(w), full(2 * LANES),
                  pl.BlockSpec((1, LANES, s), lambda b, i: (b, 0, 0)),
                  blk(qi2.shape[2]), full(LANES),
                  pl.BlockSpec((1, 8, Q_BLOCK), lambda b, i: (b, 0, i))],
        out_specs=blk(w),
        out_shape=jax.ShapeDtypeStruct((bsz, s, w), F32),
        scratch_shapes=[pltpu.VMEM((s, LANES), I32),
                        pltpu.VMEM((n_heads * Q_BLOCK, 2 * LANES), BF16),
                        pltpu.VMEM((LANES, n_heads * Q_BLOCK), F32),
                        pltpu.VMEM((1, n_heads * Q_BLOCK), F32)],
        compiler_params=pltpu.CompilerParams(
            dimension_semantics=("parallel", "arbitrary"), vmem_limit_bytes=VMEM_LIMIT),
        name="dsa_attention",
    )(dq, kda, vd1t, qi2, ki1, wt)


def _out_kernel(x_ref, osb_ref, odsa_ref, mod_ref, gsb_ref, gdsa_ref, wo_ref, o_ref, *, sb_w):
    a = _rms(osb_ref[0], gsb_ref[...]).astype(BF16)
    b = _rms(odsa_ref[0], gdsa_ref[...]).astype(BF16)
    y = _dot(a, wo_ref[0:sb_w, :]) + _dot(b, wo_ref[sb_w:, :])
    o_ref[0] = x_ref[0] + mod_ref[0][2:3] * y


def _out_proj(x, osb, odsa, mod, gsb, gdsa, wo, tm=512):
    bsz, s, d = x.shape
    sb_w, dsa_w = osb.shape[2], odsa.shape[2]
    row = lambda w: pl.BlockSpec((1, tm, w), lambda b, i: (b, i, 0))
    full = lambda shape: pl.BlockSpec(shape, lambda b, i: (0,) * len(shape))
    return pl.pallas_call(
        functools.partial(_out_kernel, sb_w=sb_w),
        grid=(bsz, s // tm),
        in_specs=[row(d), row(sb_w), row(dsa_w),
                  pl.BlockSpec((1, 6, d), lambda b, i: (b, 0, 0)),
                  full((1, sb_w)), full((1, dsa_w)), full((sb_w + dsa_w, d))],
        out_specs=row(d),
        out_shape=jax.ShapeDtypeStruct((bsz, s, d), F32),
        compiler_params=pltpu.CompilerParams(
            dimension_semantics=("parallel", "parallel"), vmem_limit_bytes=VMEM_LIMIT),
        name="out_proj",
    )(x, osb, odsa, mod, gsb, gdsa, wo)


def _ffn_kernel(x_ref, mod_ref, g_ref, wg_ref, wu_ref, wd_ref, fg_ref, fmod_ref, o_ref,
                h_ref, acc_ref, *, final):
    j = pl.program_id(2)

    @pl.when(j == 0)
    def _():
        mod = mod_ref[0]
        h = _rms(x_ref[0], g_ref[...]) * (1.0 + mod[4:5]) + mod[3:4]
        h_ref[...] = h.astype(BF16)

    h = h_ref[...]
    g = _dot(h, wg_ref[...])
    u = _dot(h, wu_ref[...])
    act = (g / (1.0 + jnp.exp(-g))) * u
    part = _dot(act.astype(BF16), wd_ref[...])

    @pl.when(j == 0)
    def _():
        acc_ref[...] = part

    @pl.when(j > 0)
    def _():
        acc_ref[...] += part

    @pl.when(j == pl.num_programs(2) - 1)
    def _():
        y = x_ref[0] + mod_ref[0][5:6] * acc_ref[...]
        if final:
            fmod = fmod_ref[0]
            y = _rms(y, fg_ref[...]) * (1.0 + fmod[1:2]) + fmod[0:1]
        o_ref[0] = y


def _ffn(x, mod, gain, wg, wu, wd, fgain, fmod, *, final, tm=1024, tf=256):
    bsz, s, d = x.shape
    dff = wg.shape[1]
    tm = min(tm, s)
    row = pl.BlockSpec((1, tm, d), lambda b, i, j: (b, i, 0))
    return pl.pallas_call(
        functools.partial(_ffn_kernel, final=final),
        grid=(bsz, s // tm, dff // tf),
        in_specs=[row,
                  pl.BlockSpec((1, 6, d), lambda b, i, j: (b, 0, 0)),
                  pl.BlockSpec((1, d), lambda b, i, j: (0, 0)),
                  pl.BlockSpec((d, tf), lambda b, i, j: (0, j)),
                  pl.BlockSpec((d, tf), lambda b, i, j: (0, j)),
                  pl.BlockSpec((tf, d), lambda b, i, j: (j, 0)),
                  pl.BlockSpec((1, d), lambda b, i, j: (0, 0)),
                  pl.BlockSpec((1, 2, d), lambda b, i, j: (b, 0, 0))],
        out_specs=row,
        out_shape=jax.ShapeDtypeStruct((bsz, s, d), F32),
        scratch_shapes=[pltpu.VMEM((tm, d), BF16), pltpu.VMEM((tm, d), F32)],
        compiler_params=pltpu.CompilerParams(
            dimension_semantics=("parallel", "parallel", "arbitrary"),
            vmem_limit_bytes=VMEM_LIMIT),
        name="swiglu_ffn",
    )(x, mod, gain, wg, wu, wd, fgain, fmod)


def kernel(x, c, w_mod, b_mod, norm1_gain, norm2_gain, w_in, kv_gain, w_uk, w_uv, sb_out_gain,
           dsa_out_gain, w_o, w_gate, w_up, w_down, w_mod_final, b_mod_final, final_gain):
    bsz, s, d = x.shape
    depth = w_in.shape[0]
    sb_w = sb_out_gain.shape[1]
    dsa_w = dsa_out_gain.shape[1]
    kv_w = kv_gain.shape[1]
    topk = min(TOPK_MAX, s // 4)
    assert s % max(SB_KC, DSA_KC) == 0 and topk % Q_BLOCK == 0
    assert HEAD_DIM * 2 == LANES and w_uk.shape[2] == HEAD_DIM and s // LANES <= 256

    mod = _mod(c, w_mod, b_mod).reshape(depth, bsz, 6, d)
    fmod = _mod(c, w_mod_final[None], b_mod_final[None]).reshape(bsz, 2, d)

    n_a = 3 * sb_w + dsa_w + kv_w
    n_idx = N_IDX_HEADS * IDX_DIM + IDX_DIM + N_IDX_HEADS
    idx_pad = N_IDX_HEADS * IDX_DIM + LANES - n_idx
    bkv = jnp.concatenate([jnp.zeros((3 * HEAD_DIM,), F32), jnp.ones((HEAD_DIM,), F32)])[None]

    for l in range(depth):
        wa = w_in[l][:, :n_a].astype(BF16)
        wi = jnp.pad(w_in[l][:, n_a:], ((0, 0), (0, idx_pad)))
        wih = wi.astype(BF16)
        wil = (wi - wih.astype(F32)).astype(BF16)
        wkv = jnp.concatenate([w_uk[l], w_uk[l], w_uv[l], jnp.zeros_like(w_uv[l])],
                              axis=1).astype(BF16)
        (sbq, sbk, sbv, dq, kda, qi2, ki1, wt, vd1t) = _in_proj(
            x, mod[l], norm1_gain[l][None], wa, wih, wil, kv_gain[l][None], wkv, bkv,
            sb_w=sb_w, dsa_w=dsa_w, kv_w=kv_w)
        osb = _sb_attention(sbq, sbk, sbv)
        odsa = _dsa_attention(dq, kda, vd1t, qi2, ki1, wt, topk=topk)
        x = _out_proj(x, osb, odsa, mod[l], sb_out_gain[l][None], dsa_out_gain[l][None],
                      w_o[l].astype(BF16))
        x = _ffn(x, mod[l], norm2_gain[l][None], w_gate[l].astype(BF16), w_up[l].astype(BF16),
                 w_down[l].astype(BF16), final_gain[None], fmod, final=(l == depth - 1))
    return x
```

```python
import functools

import jax
import jax.numpy as jnp
from jax import lax
from jax.experimental import pallas as pl
from jax.experimental.pallas import tpu as pltpu

F32 = jnp.float32
BF16 = jnp.bfloat16
I32 = jnp.int32

HEAD_DIM = 64
LANES = 128
Q_BLOCK = 128
N_IDX_HEADS = 4
IDX_DIM = 64
TOPK_MAX = 256
RMS_EPS = 1e-6
INT_MIN = -2 ** 31
NEG_BIG = -1e30
VMEM_LIMIT = 52 * 1024 * 1024
SB_KC = 512
SB_ROW_TILE = 64
DSA_KC = 512


def _nt(a, b):
    return lax.dot_general(a, b, (((1,), (1,)), ((), ())), preferred_element_type=F32)


def _dot(a, b):
    return jnp.dot(a, b, preferred_element_type=F32)


def _split(x):
    hi = x.astype(BF16)
    lo = (x - hi.astype(F32)).astype(BF16)
    return hi, lo


def _rms(x, gain):
    ms = jnp.mean(x * x, axis=-1, keepdims=True)
    return x * lax.rsqrt(ms + RMS_EPS) * gain


def _mod_kernel(c_ref, w_ref, b_ref, o_ref):
    c = c_ref[...]
    ca = c / (1.0 + jnp.exp(-c))
    ch, cl = _split(ca)
    wh, wl = _split(w_ref[0])
    o_ref[0] = _dot(ch, wh) + _dot(ch, wl) + _dot(cl, wh) + b_ref[0]


def _mod(c, w, b, tn=1024):
    nl, d, n = w.shape
    bsz = c.shape[0]
    return pl.pallas_call(
        _mod_kernel,
        grid=(nl, n // tn),
        in_specs=[pl.BlockSpec((bsz, d), lambda l, j: (0, 0)),
                  pl.BlockSpec((1, d, tn), lambda l, j: (l, 0, j)),
                  pl.BlockSpec((1, 1, tn), lambda l, j: (l, 0, j))],
        out_specs=pl.BlockSpec((1, bsz, tn), lambda l, j: (l, 0, j)),
        out_shape=jax.ShapeDtypeStruct((nl, bsz, n), F32),
        name="adaln_mod",
    )(c, w, b.reshape(nl, 1, n))


def _in_kernel(x_ref, mod_ref, g_ref, wa_ref, wih_ref, wil_ref, kvg_ref, wkv_ref, bkv_ref,
               sbq_ref, sbk_ref, sbv_ref, dq_ref, kda_ref, qi2_ref, ki1_ref, wt_ref, vd1t_ref,
               *, sb_w, dsa_w, kv_w):
    tm = x_ref.shape[1]
    x = x_ref[0]
    mod = mod_ref[0]
    h = _rms(x, g_ref[...]) * (1.0 + mod[1:2]) + mod[0:1]
    hh, hl = _split(h)
    main = _dot(hh, wa_ref[...])
    scale = HEAD_DIM ** -0.5
    sbq_ref[0] = (main[:, 0:sb_w] * scale).astype(BF16)
    sbk_ref[0] = main[:, sb_w:2 * sb_w].astype(BF16)
    sbv_ref[0] = main[:, 2 * sb_w:3 * sb_w].astype(BF16)
    o = 3 * sb_w
    dq_ref[0] = (main[:, o:o + dsa_w] * scale).astype(BF16)
    o += dsa_w
    kv = _rms(main[:, o:o + kv_w], kvg_ref[...])
    kvp = _dot(kv.astype(BF16), wkv_ref[...]) + bkv_ref[...]
    lane = lax.broadcasted_iota(I32, (tm, LANES), 1)
    pos = pl.program_id(1) * tm + lax.broadcasted_iota(I32, (tm, LANES), 0)
    posf = jnp.where(lane == 0, pos >> 7, jnp.where(lane == 1, pos & (LANES - 1), 0)).astype(F32)
    kda_ref[0, :, 0:LANES] = kvp[:, 0:LANES].astype(BF16)
    kda_ref[0, :, LANES:2 * LANES] = posf.astype(BF16)
    vd1t_ref[0] = kvp[:, LANES:2 * LANES].T.astype(BF16)
    wih = wih_ref[...]
    idx = _dot(hh, wih) + _dot(hh, wil_ref[...]) + _dot(hl, wih)
    first = lane < IDX_DIM
    for pr in range(N_IDX_HEADS // 2):
        pair = idx[:, pr * LANES:(pr + 1) * LANES] * (IDX_DIM ** -0.5)
        rolled = pltpu.roll(pair, IDX_DIM, 1)
        for half in range(2):
            dup = jnp.where(first, pair, rolled) if half == 0 else jnp.where(first, rolled, pair)
            qh, ql = _split(dup)
            hd = 2 * pr + half
            qi2_ref[0, :, 2 * hd * LANES:(2 * hd + 1) * LANES] = qh
            qi2_ref[0, :, (2 * hd + 1) * LANES:(2 * hd + 2) * LANES] = jnp.where(
                first, ql, jnp.zeros_like(ql))
    nq = N_IDX_HEADS * IDX_DIM
    kw = idx[:, nq:nq + LANES]
    kh, kl = _split(jnp.where(first, kw, pltpu.roll(kw, IDX_DIM, 1)))
    ki1_ref[0] = jnp.where(first, kh, kl)
    kwt = kw.T
    wt_ref[0] = kwt[IDX_DIM:IDX_DIM + 8, :] * (N_IDX_HEADS ** -0.5)


def _in_proj(x, mod, gain, wa, wih, wil, kvg, wkv, bkv, *, sb_w, dsa_w, kv_w, tm=512):
    bsz, s, d = x.shape
    na = wa.shape[1]
    ni = wih.shape[1]
    nkv = wkv.shape[1]
    full = lambda shape: pl.BlockSpec(shape, lambda b, i: (0,) * len(shape))
    row = lambda w: pl.BlockSpec((1, tm, w), lambda b, i: (b, i, 0))
    outs = [
        (sb_w, BF16), (sb_w, BF16), (sb_w, BF16), (dsa_w, BF16),
        (2 * LANES, BF16), (2 * LANES * N_IDX_HEADS, BF16), (LANES, BF16),
    ]
    out_shape = [jax.ShapeDtypeStruct((bsz, s, w), dt) for w, dt in outs]
    out_specs = [row(w) for w, _ in outs]
    for rows, dt in ((8, F32), (LANES, BF16)):
        out_shape.append(jax.ShapeDtypeStruct((bsz, rows, s), dt))
        out_specs.append(pl.BlockSpec((1, rows, tm), lambda b, i: (b, 0, i)))
    return pl.pallas_call(
        functools.partial(_in_kernel, sb_w=sb_w, dsa_w=dsa_w, kv_w=kv_w),
        grid=(bsz, s // tm),
        in_specs=[row(d),
                  pl.BlockSpec((1, 6, d), lambda b, i: (b, 0, 0)),
                  full((1, d)), full((d, na)), full((d, ni)), full((d, ni)),
                  full((1, kv_w)), full((kv_w, nkv)), full((1, nkv))],
        out_specs=out_specs,
        out_shape=out_shape,
        compiler_params=pltpu.CompilerParams(
            dimension_semantics=("parallel", "parallel"), vmem_limit_bytes=VMEM_LIMIT),
        name="in_proj",
    )(x, mod, gain, wa, wih, wil, kvg, wkv, bkv)


def _sb_kernel(q_ref, k_ref, v_ref, o_ref, car_ref, *, npair):
    i = pl.program_id(1)
    t = Q_BLOCK
    kc = SB_KC
    nsub = kc // LANES
    lane = lax.broadcasted_iota(I32, (t, LANES), 1)
    first = lane < HEAD_DIM
    zb = jnp.zeros((t, LANES), BF16)
    r2 = lax.broadcasted_iota(I32, (LANES, 2 * LANES), 0)
    c2 = lax.broadcasted_iota(I32, (LANES, 2 * LANES), 1)
    uo = jnp.where((c2 >= LANES) | (r2 > c2), 1.0, 0.0).astype(BF16)
    ct = i // nsub
    rt = SB_ROW_TILE

    def stage_scores(pr, c, ns):
        cols = slice(pr * LANES, (pr + 1) * LANES)
        start = pl.multiple_of(c * kc, kc)
        q2 = q_ref[0, :, cols]
        qs = jnp.concatenate([jnp.where(first, q2, zb), jnp.where(first, zb, q2)], axis=0)
        z = _nt(qs, k_ref[0, pl.ds(start, ns * LANES), cols])
        return jnp.concatenate([z[:, j * LANES:(j + 1) * LANES] for j in range(ns)], axis=0)

    def causal_tile(r0):
        rr = (r0 + lax.broadcasted_iota(I32, (rt, LANES), 0)) & (t - 1)
        return lax.broadcasted_iota(I32, (rt, LANES), 1) < rr

    def stage_cumsum(u, top, ns):
        ds, sps = [], []
        for r0 in range(0, ns * 2 * t, rt):
            ut = u[r0:r0 + rt]
            nabs = lax.bitcast_convert_type(lax.bitcast_convert_type(ut, I32) | INT_MIN, F32)
            sp = jnp.maximum(ut, 0.0) + jnp.log(1.0 + jnp.exp(nabs))
            ds.append(ut - sp)
            if top and r0 >= (ns - 1) * 2 * t:
                sp = jnp.where(causal_tile(r0), sp, 0.0)
            sps.append(sp.astype(BF16))
        spb = jnp.concatenate(sps, axis=0)
        half = ns * t
        cs = jnp.concatenate([_dot(spb[:half], uo), _dot(spb[half:], uo)], axis=0)
        return ds, cs

    def stage_output(pr, c, ds, cs, top, ns):
        cols = slice(pr * LANES, (pr + 1) * LANES)
        start = pl.multiple_of(c * kc, kc)
        v2 = v_ref[0, pl.ds(start, ns * LANES), cols]
        firstk = lax.broadcasted_iota(I32, (ns * LANES, LANES), 1) < HEAD_DIM
        zk = jnp.zeros((ns * LANES, LANES), BF16)
        ntile = 2 * t // rt
        runs = [jnp.zeros((rt, LANES), F32) if top else car_ref[pr, k * rt:(k + 1) * rt]
                for k in range(ntile)]
        pieces = [[None] * ntile for _ in range(ns)]
        for j in reversed(range(ns)):
            for k in range(ntile):
                r0 = j * 2 * t + k * rt
                a = jnp.exp(ds[r0 // rt] - (cs[r0:r0 + rt, :LANES] + runs[k]))
                if top and j == ns - 1:
                    a = jnp.where(causal_tile(r0), a, 0.0)
                pieces[j][k] = a.astype(BF16)
                runs[k] = runs[k] + cs[r0:r0 + rt, LANES:]
        for k in range(ntile):
            car_ref[pr, k * rt:(k + 1) * rt] = runs[k]
        pieces = [jnp.concatenate(p, axis=0) for p in pieces]
        a_e = jnp.concatenate([p[:t] for p in pieces], axis=1)
        a_o = jnp.concatenate([p[t:] for p in pieces], axis=1)
        pv = _dot(a_e, jnp.where(firstk, v2, zk)) + _dot(a_o, jnp.where(firstk, zk, v2))
        if top:
            o_ref[0, :, cols] = pv
        else:
            o_ref[0, :, cols] += pv

    def all_pairs(c, top, ns):
        us, mids = {}, {}
        for step in range(npair + 2):
            if step < npair:
                us[step] = stage_scores(step, c, ns)
            if 0 <= step - 1 < npair:
                mids[step - 1] = stage_cumsum(us.pop(step - 1), top, ns)
            if 0 <= step - 2 < npair:
                stage_output(step - 2, c, *mids.pop(step - 2), top, ns)

    for r in range(nsub):
        @pl.when(i % nsub == r)
        def _(r=r):
            all_pairs(ct, True, r + 1)

    def body(j, carry):
        all_pairs(ct - 1 - j, False, nsub)
        return carry

    lax.fori_loop(0, ct, body, 0)


def _sb_attention(q, k, v):
    bsz, s, w = q.shape
    npair = w // LANES
    nq = s // Q_BLOCK
    return pl.pallas_call(
        functools.partial(_sb_kernel, npair=npair),
        grid=(bsz, nq),
        in_specs=[pl.BlockSpec((1, Q_BLOCK, w), lambda b, i: (b, i, 0)),
                  pl.BlockSpec((1, s, w), lambda b, i: (b, 0, 0)),
                  pl.BlockSpec((1, s, w), lambda b, i: (b, 0, 0))],
        out_specs=pl.BlockSpec((1, Q_BLOCK, w), lambda b, i: (b, i, 0)),
        out_shape=jax.ShapeDtypeStruct((bsz, s, w), F32),
        scratch_shapes=[pltpu.VMEM((npair, 2 * Q_BLOCK, LANES), F32)],
        compiler_params=pltpu.CompilerParams(
            dimension_semantics=("parallel", "arbitrary"), vmem_limit_bytes=VMEM_LIMIT),
        name="sb_attention",
    )(q, k, v)


def _sortable(x):
    b = lax.bitcast_convert_type(x + 0.0, I32)
    return b ^ ((b >> 31) & 0x7FFFFFFF)


def _dsa_kernel(dq_ref, kda_ref, vd1t_ref, qi2_ref, ki1_ref, wt_ref, o_ref,
                key_ref, qa_ref, acc_ref, m_ref, *, topk, n_heads):
    i = pl.program_id(1)
    t = Q_BLOCK
    kc = DSA_KC
    n_kc = i // (kc // t) + 1
    kf = float(topk)
    lane = lax.broadcasted_iota(I32, (t, LANES), 1)
    row = lax.broadcasted_iota(I32, (t, LANES), 0)
    first = lane < HEAD_DIM

    wt = wt_ref[0]
    rowlane = (lax.broadcasted_iota(I32, (kc, LANES), 0)
               - lax.broadcasted_iota(I32, (kc, LANES), 1))

    def score_chunk(c, carry):
        start = pl.multiple_of(c * kc, kc)
        k1 = ki1_ref[0, pl.ds(start, kc), :]
        lhs = jnp.concatenate([k1, k1], axis=1)
        score = None
        for pr in range(N_IDX_HEADS // 2):
            w2 = jnp.concatenate(
                [qi2_ref[0, :, 2 * hd * LANES:(2 * hd + 2) * LANES] for hd in (2 * pr, 2 * pr + 1)],
                axis=0)
            lg = _nt(lhs, w2)
            for half in range(2):
                hd = 2 * pr + half
                term = jnp.maximum(lg[:, half * LANES:(half + 1) * LANES], 0.0) * wt[hd:hd + 1, :]
                score = term if score is None else score + term
        key = _sortable(score)
        key_ref[pl.ds(start, kc), :] = jnp.where(rowlane > i * t - c * kc, INT_MIN, key)
        return carry

    lax.fori_loop(0, n_kc, score_chunk, 0)

    def count(pred):
        def body(c, acc):
            start = pl.multiple_of(c * kc, kc)
            m = jnp.where(pred(key_ref[pl.ds(start, kc), :]), 1.0, 0.0)
            m = m.reshape(8, kc // 8, LANES).sum(axis=0)
            return acc + m.reshape(kc // 64, 8, LANES).sum(axis=0)
        part = lax.fori_loop(0, n_kc, body, jnp.zeros((8, LANES), F32))
        return jnp.sum(part, axis=0, keepdims=True)

    searching = (i + 1) * t > topk

    def bit_step(bi, lo):
        cand = lo + lax.shift_left(jnp.int32(1), 31 - bi)
        cnt = count(lambda blk: blk >= cand)
        return jnp.where(cnt >= kf, cand, lo)

    thr = lax.fori_loop(0, jnp.where(searching, 32, 0), bit_step,
                        jnp.full((1, LANES), INT_MIN, I32))
    cnt_ge = count(lambda blk: blk >= thr)

    @pl.when((jnp.max(cnt_ge) > kf) & searching)
    def _():
        need = kf - count(lambda blk: blk > thr)
        sl = jnp.where(lane < row, 1.0, 0.0).astype(BF16)
        nsub = kc // t

        def body(c, before):
            start = pl.multiple_of(c * kc, kc)
            subs = [key_ref[pl.ds(start + j * t, t), :] for j in range(nsub)]
            eqfs = [jnp.where(sub == thr, 1.0, 0.0) for sub in subs]
            ranks = [_dot(sl, eqf.astype(BF16)) for eqf in eqfs]
            for j in range(nsub):
                demote = (subs[j] == thr) & (ranks[j] + before >= need)
                key_ref[pl.ds(start + j * t, t), :] = jnp.where(demote, INT_MIN, subs[j])
                before = before + jnp.sum(eqfs[j], axis=0, keepdims=True)
            return before

        lax.fori_loop(0, n_kc, body, jnp.zeros((1, LANES), F32))

    thr_eff = jnp.maximum(thr, INT_MIN + 1)

    zb = jnp.zeros((t, LANES), BF16)
    for hd in range(n_heads):
        q2 = dq_ref[0, :, (hd // 2) * LANES:(hd // 2 + 1) * LANES]
        slope = 2.0 ** (-8.0 * (hd + 1) / n_heads)
        aug = jnp.where(lane == 0, LANES * slope, jnp.where(lane == 1, slope, 0.0)).astype(BF16)
        qa_ref[hd * t:(hd + 1) * t, 0:LANES] = (
            jnp.where(first, q2, zb) if hd % 2 == 0 else jnp.where(first, zb, q2))
        qa_ref[hd * t:(hd + 1) * t, LANES:2 * LANES] = aug
    m_ref[...] = jnp.full(m_ref.shape, NEG_BIG, F32)
    acc_ref[...] = jnp.zeros(acc_ref.shape, F32)

    def attn_chunk(c, carry):
        start = pl.multiple_of(c * kc, kc)
        kd = kda_ref[0, pl.ds(start, kc), :]
        vt = vd1t_ref[0, :, pl.ds(start, kc)]
        bias = jnp.where(key_ref[pl.ds(start, kc), :] >= thr_eff, 0.0, -jnp.inf)
        sc = _nt(kd, qa_ref[...]) + jnp.concatenate([bias] * n_heads, axis=1)
        m_old = m_ref[...]
        m_new = jnp.maximum(m_old, jnp.max(sc, axis=0, keepdims=True))
        h = kc // 2
        pv = (_dot(vt[:, :h], jnp.exp(sc[:h] - m_new).astype(BF16))
              + _dot(vt[:, h:], jnp.exp(sc[h:] - m_new).astype(BF16)))
        acc_ref[...] = acc_ref[...] * jnp.exp(m_old - m_new) + pv
        m_ref[...] = m_new
        return carry

    lax.fori_loop(0, n_kc, attn_chunk, 0)

    for pr in range(n_heads // 2):
        a_e = acc_ref[:, (2 * pr) * t:(2 * pr + 1) * t]
        a_o = acc_ref[:, (2 * pr + 1) * t:(2 * pr + 2) * t]
        r = jnp.concatenate([a_e[:HEAD_DIM] / a_e[HEAD_DIM:], a_o[:HEAD_DIM] / a_o[HEAD_DIM:]], axis=0)
        o_ref[0, :, pr * LANES:(pr + 1) * LANES] = r.T


def _dsa_attention(dq, kda, vd1t, qi2, ki1, wt, *, topk):
    bsz, s, w = dq.shape
    nq = s // Q_BLOCK
    n_heads = w // HEAD_DIM
    blk = lambda wd: pl.BlockSpec((1, Q_BLOCK, wd), lambda b, i: (b, i, 0))
    full = lambda wd: pl.BlockSpec((1, s, wd), lambda b, i: (b, 0, 0))
    return pl.pallas_call(
        functools.partial(_dsa_kernel, topk=topk, n_heads=n_heads),
        grid=(bsz, nq),
        in_specs=[blk(w), full(2 * LANES),
                  pl.BlockSpec((1, LANES, s), lambda b, i: (b, 0, 0)),
                  blk(qi2.shape[2]), full(LANES),
                  pl.BlockSpec((1, 8, Q_BLOCK), lambda b, i: (b, 0, i))],
        out_specs=blk(w),
        out_shape=jax.ShapeDtypeStruct((bsz, s, w), F32),
        scratch_shapes=[pltpu.VMEM((s, LANES), I32),
                        pltpu.VMEM((n_heads * Q_BLOCK, 2 * LANES), BF16),
                        pltpu.VMEM((LANES, n_heads * Q_BLOCK), F32),
                        pltpu.VMEM((1, n_heads * Q_BLOCK), F32)],
        compiler_params=pltpu.CompilerParams(
            dimension_semantics=("parallel", "arbitrary"), vmem_limit_bytes=VMEM_LIMIT),
        name="dsa_attention",
    )(dq, kda, vd1t, qi2, ki1, wt)


def _out_kernel(x_ref, osb_ref, odsa_ref, mod_ref, gsb_ref, gdsa_ref, wo_ref, o_ref, *, sb_w):
    a = _rms(osb_ref[0], gsb_ref[...]).astype(BF16)
    b = _rms(odsa_ref[0], gdsa_ref[...]).astype(BF16)
    y = _dot(a, wo_ref[0:sb_w, :]) + _dot(b, wo_ref[sb_w:, :])
    o_ref[0] = x_ref[0] + mod_ref[0][2:3] * y


def _out_proj(x, osb, odsa, mod, gsb, gdsa, wo, tm=512):
    bsz, s, d = x.shape
    sb_w, dsa_w = osb.shape[2], odsa.shape[2]
    row = lambda w: pl.BlockSpec((1, tm, w), lambda b, i: (b, i, 0))
    full = lambda shape: pl.BlockSpec(shape, lambda b, i: (0,) * len(shape))
    return pl.pallas_call(
        functools.partial(_out_kernel, sb_w=sb_w),
        grid=(bsz, s // tm),
        in_specs=[row(d), row(sb_w), row(dsa_w),
                  pl.BlockSpec((1, 6, d), lambda b, i: (b, 0, 0)),
                  full((1, sb_w)), full((1, dsa_w)), full((sb_w + dsa_w, d))],
        out_specs=row(d),
        out_shape=jax.ShapeDtypeStruct((bsz, s, d), F32),
        compiler_params=pltpu.CompilerParams(
            dimension_semantics=("parallel", "parallel"), vmem_limit_bytes=VMEM_LIMIT),
        name="out_proj",
    )(x, osb, odsa, mod, gsb, gdsa, wo)


def _ffn_kernel(x_ref, mod_ref, g_ref, wg_ref, wu_ref, wd_ref, fg_ref, fmod_ref, o_ref,
                h_ref, acc_ref, *, final):
    j = pl.program_id(2)

    @pl.when(j == 0)
    def _():
        mod = mod_ref[0]
        h = _rms(x_ref[0], g_ref[...]) * (1.0 + mod[4:5]) + mod[3:4]
        h_ref[...] = h.astype(BF16)

    h = h_ref[...]
    g = _dot(h, wg_ref[...])
    u = _dot(h, wu_ref[...])
    act = (g / (1.0 + jnp.exp(-g))) * u
    part = _dot(act.astype(BF16), wd_ref[...])

    @pl.when(j == 0)
    def _():
        acc_ref[...] = part

    @pl.when(j > 0)
    def _():
        acc_ref[...] += part

    @pl.when(j == pl.num_programs(2) - 1)
    def _():
        y = x_ref[0] + mod_ref[0][5:6] * acc_ref[...]
        if final:
            fmod = fmod_ref[0]
            y = _rms(y, fg_ref[...]) * (1.0 + fmod[1:2]) + fmod[0:1]
        o_ref[0] = y


def _ffn(x, mod, gain, wg, wu, wd, fgain, fmod, *, final, tm=512, tf=1408):
    bsz, s, d = x.shape
    dff = wg.shape[1]
    tm = min(tm, s)
    row = pl.BlockSpec((1, tm, d), lambda b, i, j: (b, i, 0))
    return pl.pallas_call(
        functools.partial(_ffn_kernel, final=final),
        grid=(bsz, s // tm, dff // tf),
        in_specs=[row,
                  pl.BlockSpec((1, 6, d), lambda b, i, j: (b, 0, 0)),
                  pl.BlockSpec((1, d), lambda b, i, j: (0, 0)),
                  pl.BlockSpec((d, tf), lambda b, i, j: (0, j)),
                  pl.BlockSpec((d, tf), lambda b, i, j: (0, j)),
                  pl.BlockSpec((tf, d), lambda b, i, j: (j, 0)),
                  pl.BlockSpec((1, d), lambda b, i, j: (0, 0)),
                  pl.BlockSpec((1, 2, d), lambda b, i, j: (b, 0, 0))],
        out_specs=row,
        out_shape=jax.ShapeDtypeStruct((bsz, s, d), F32),
        scratch_shapes=[pltpu.VMEM((tm, d), BF16), pltpu.VMEM((tm, d), F32)],
        compiler_params=pltpu.CompilerParams(
            dimension_semantics=("parallel", "parallel", "arbitrary"),
            vmem_limit_bytes=VMEM_LIMIT),
        name="swiglu_ffn",
    )(x, mod, gain, wg, wu, wd, fgain, fmod)


def kernel(x, c, w_mod, b_mod, norm1_gain, norm2_gain, w_in, kv_gain, w_uk, w_uv, sb_out_gain,
           dsa_out_gain, w_o, w_gate, w_up, w_down, w_mod_final, b_mod_final, final_gain):
    bsz, s, d = x.shape
    depth = w_in.shape[0]
    sb_w = sb_out_gain.shape[1]
    dsa_w = dsa_out_gain.shape[1]
    kv_w = kv_gain.shape[1]
    topk = min(TOPK_MAX, s // 4)
    assert s % max(SB_KC, DSA_KC) == 0 and topk % Q_BLOCK == 0
    assert HEAD_DIM * 2 == LANES and w_uk.shape[2] == HEAD_DIM and s // LANES <= 256

    mod = _mod(c, w_mod, b_mod).reshape(depth, bsz, 6, d)
    fmod = _mod(c, w_mod_final[None], b_mod_final[None]).reshape(bsz, 2, d)

    n_a = 3 * sb_w + dsa_w + kv_w
    n_idx = N_IDX_HEADS * IDX_DIM + IDX_DIM + N_IDX_HEADS
    idx_pad = N_IDX_HEADS * IDX_DIM + LANES - n_idx
    bkv = jnp.concatenate([jnp.zeros((3 * HEAD_DIM,), F32), jnp.ones((HEAD_DIM,), F32)])[None]

    for l in range(depth):
        wa = w_in[l][:, :n_a].astype(BF16)
        wi = jnp.pad(w_in[l][:, n_a:], ((0, 0), (0, idx_pad)))
        wih = wi.astype(BF16)
        wil = (wi - wih.astype(F32)).astype(BF16)
        wkv = jnp.concatenate([w_uk[l], w_uk[l], w_uv[l], jnp.zeros_like(w_uv[l])],
                              axis=1).astype(BF16)
        (sbq, sbk, sbv, dq, kda, qi2, ki1, wt, vd1t) = _in_proj(
            x, mod[l], norm1_gain[l][None], wa, wih, wil, kv_gain[l][None], wkv, bkv,
            sb_w=sb_w, dsa_w=dsa_w, kv_w=kv_w)
        osb = _sb_attention(sbq, sbk, sbv)
        odsa = _dsa_attention(dq, kda, vd1t, qi2, ki1, wt, topk=topk)
        x = _out_proj(x, osb, odsa, mod[l], sb_out_gain[l][None], dsa_out_gain[l][None],
                      w_o[l].astype(BF16))
        x = _ffn(x, mod[l], norm2_gain[l][None], w_gate[l].astype(BF16), w_up[l].astype(BF16),
                 w_down[l].astype(BF16), final_gain[None], fmod, final=(l == depth - 1))
    return x
```

```python
import functools

import jax
import jax.numpy as jnp
from jax import lax
from jax.experimental import pallas as pl
from jax.experimental.pallas import tpu as pltpu

F32 = jnp.float32
BF16 = jnp.bfloat16
I32 = jnp.int32

HEAD_DIM = 64
LANES = 128
Q_BLOCK = 128
N_IDX_HEADS = 4
IDX_DIM = 64
TOPK_MAX = 256
RMS_EPS = 1e-6
INT_MIN = -2 ** 31
NEG_BIG = -1e30
VMEM_LIMIT = 52 * 1024 * 1024
SB_KC = 512
SB_ROW_TILE = 64
DSA_KC = 512


def _nt(a, b):
    return lax.dot_general(a, b, (((1,), (1,)), ((), ())), preferred_element_type=F32)


def _dot(a, b):
    return jnp.dot(a, b, preferred_element_type=F32)


def _split(x):
    hi = x.astype(BF16)
    lo = (x - hi.astype(F32)).astype(BF16)
    return hi, lo


def _rms(x, gain):
    ms = jnp.mean(x * x, axis=-1, keepdims=True)
    return x * lax.rsqrt(ms + RMS_EPS) * gain


def _mod_kernel(c_ref, w_ref, b_ref, o_ref):
    c = c_ref[...]
    ca = c / (1.0 + jnp.exp(-c))
    ch, cl = _split(ca)
    wh, wl = _split(w_ref[0])
    o_ref[0] = _dot(ch, wh) + _dot(ch, wl) + _dot(cl, wh) + b_ref[0]


def _mod(c, w, b, tn=1024):
    nl, d, n = w.shape
    bsz = c.shape[0]
    return pl.pallas_call(
        _mod_kernel,
        grid=(nl, n // tn),
        in_specs=[pl.BlockSpec((bsz, d), lambda l, j: (0, 0)),
                  pl.BlockSpec((1, d, tn), lambda l, j: (l, 0, j)),
                  pl.BlockSpec((1, 1, tn), lambda l, j: (l, 0, j))],
        out_specs=pl.BlockSpec((1, bsz, tn), lambda l, j: (l, 0, j)),
        out_shape=jax.ShapeDtypeStruct((nl, bsz, n), F32),
        name="adaln_mod",
    )(c, w, b.reshape(nl, 1, n))


def _in_kernel(x_ref, mod_ref, g_ref, wa_ref, wih_ref, wil_ref, kvg_ref, wkv_ref, bkv_ref,
               sbq_ref, sbk_ref, sbv_ref, dq_ref, kda_ref, qi2_ref, ki1_ref, wt_ref, vd1t_ref,
               *, sb_w, dsa_w, kv_w):
    tm = x_ref.shape[1]
    x = x_ref[0]
    mod = mod_ref[0]
    h = _rms(x, g_ref[...]) * (1.0 + mod[1:2]) + mod[0:1]
    hh, hl = _split(h)
    main = _dot(hh, wa_ref[...])
    scale = HEAD_DIM ** -0.5
    sbq_ref[0] = (main[:, 0:sb_w] * scale).astype(BF16)
    sbk_ref[0] = main[:, sb_w:2 * sb_w].astype(BF16)
    sbv_ref[0] = main[:, 2 * sb_w:3 * sb_w].astype(BF16)
    o = 3 * sb_w
    dq_ref[0] = (main[:, o:o + dsa_w] * scale).astype(BF16)
    o += dsa_w
    kv = _rms(main[:, o:o + kv_w], kvg_ref[...])
    kvp = _dot(kv.astype(BF16), wkv_ref[...]) + bkv_ref[...]
    lane = lax.broadcasted_iota(I32, (tm, LANES), 1)
    pos = pl.program_id(1) * tm + lax.broadcasted_iota(I32, (tm, LANES), 0)
    posf = jnp.where(lane == 0, pos >> 7, jnp.where(lane == 1, pos & (LANES - 1), 0)).astype(F32)
    kda_ref[0, :, 0:LANES] = kvp[:, 0:LANES].astype(BF16)
    kda_ref[0, :, LANES:2 * LANES] = posf.astype(BF16)
    vd1t_ref[0] = kvp[:, LANES:2 * LANES].T.astype(BF16)
    wih = wih_ref[...]
    idx = _dot(hh, wih) + _dot(hh, wil_ref[...]) + _dot(hl, wih)
    first = lane < IDX_DIM
    for pr in range(N_IDX_HEADS // 2):
        pair = idx[:, pr * LANES:(pr + 1) * LANES] * (IDX_DIM ** -0.5)
        rolled = pltpu.roll(pair, IDX_DIM, 1)
        for half in range(2):
            dup = jnp.where(first, pair, rolled) if half == 0 else jnp.where(first, rolled, pair)
            qh, ql = _split(dup)
            hd = 2 * pr + half
            qi2_ref[0, :, 2 * hd * LANES:(2 * hd + 1) * LANES] = qh
            qi2_ref[0, :, (2 * hd + 1) * LANES:(2 * hd + 2) * LANES] = jnp.where(
                first, ql, jnp.zeros_like(ql))
    nq = N_IDX_HEADS * IDX_DIM
    kw = idx[:, nq:nq + LANES]
    kh, kl = _split(jnp.where(first, kw, pltpu.roll(kw, IDX_DIM, 1)))
    ki1_ref[0] = jnp.where(first, kh, kl)
    kwt = kw.T
    wt_ref[0] = kwt[IDX_DIM:IDX_DIM + 8, :] * (N_IDX_HEADS ** -0.5)


def _in_proj(x, mod, gain, wa, wih, wil, kvg, wkv, bkv, *, sb_w, dsa_w, kv_w, tm=512):
    bsz, s, d = x.shape
    na = wa.shape[1]
    ni = wih.shape[1]
    nkv = wkv.shape[1]
    full = lambda shape: pl.BlockSpec(shape, lambda b, i: (0,) * len(shape))
    row = lambda w: pl.BlockSpec((1, tm, w), lambda b, i: (b, i, 0))
    outs = [
        (sb_w, BF16), (sb_w, BF16), (sb_w, BF16), (dsa_w, BF16),
        (2 * LANES, BF16), (2 * LANES * N_IDX_HEADS, BF16), (LANES, BF16),
    ]
    out_shape = [jax.ShapeDtypeStruct((bsz, s, w), dt) for w, dt in outs]
    out_specs = [row(w) for w, _ in outs]
    for rows, dt in ((8, F32), (LANES, BF16)):
        out_shape.append(jax.ShapeDtypeStruct((bsz, rows, s), dt))
        out_specs.append(pl.BlockSpec((1, rows, tm), lambda b, i: (b, 0, i)))
    return pl.pallas_call(
        functools.partial(_in_kernel, sb_w=sb_w, dsa_w=dsa_w, kv_w=kv_w),
        grid=(bsz, s // tm),
        in_specs=[row(d),
                  pl.BlockSpec((1, 6, d), lambda b, i: (b, 0, 0)),
                  full((1, d)), full((d, na)), full((d, ni)), full((d, ni)),
                  full((1, kv_w)), full((kv_w, nkv)), full((1, nkv))],
        out_specs=out_specs,
        out_shape=out_shape,
        compiler_params=pltpu.CompilerParams(
            dimension_semantics=("parallel", "parallel"), vmem_limit_bytes=VMEM_LIMIT),
        name="in_proj",
    )(x, mod, gain, wa, wih, wil, kvg, wkv, bkv)


def _sb_kernel(q_ref, k_ref, v_ref, o_ref, car_ref, *, npair):
    i = pl.program_id(1)
    t = Q_BLOCK
    kc = SB_KC
    nsub = kc // LANES
    lane = lax.broadcasted_iota(I32, (t, LANES), 1)
    first = lane < HEAD_DIM
    zb = jnp.zeros((t, LANES), BF16)
    r2 = lax.broadcasted_iota(I32, (LANES, 2 * LANES), 0)
    c2 = lax.broadcasted_iota(I32, (LANES, 2 * LANES), 1)
    uo = jnp.where((c2 >= LANES) | (r2 > c2), 1.0, 0.0).astype(BF16)
    ct = i // nsub
    rt = SB_ROW_TILE

    def stage_scores(pr, c, ns):
        cols = slice(pr * LANES, (pr + 1) * LANES)
        start = pl.multiple_of(c * kc, kc)
        q2 = q_ref[0, :, cols]
        qs = jnp.concatenate([jnp.where(first, q2, zb), jnp.where(first, zb, q2)], axis=0)
        z = _nt(qs, k_ref[0, pl.ds(start, ns * LANES), cols])
        return jnp.concatenate([z[:, j * LANES:(j + 1) * LANES] for j in range(ns)], axis=0)

    def causal_tile(r0):
        rr = (r0 + lax.broadcasted_iota(I32, (rt, LANES), 0)) & (t - 1)
        return lax.broadcasted_iota(I32, (rt, LANES), 1) < rr

    def stage_cumsum(u, top, ns):
        ds, sps = [], []
        for r0 in range(0, ns * 2 * t, rt):
            ut = u[r0:r0 + rt]
            sp = jnp.maximum(ut, 0.0) + jnp.log(1.0 + jnp.exp(-jnp.abs(ut)))
            ds.append(ut - sp)
            if top and r0 >= (ns - 1) * 2 * t:
                sp = jnp.where(causal_tile(r0), sp, 0.0)
            sps.append(sp.astype(BF16))
        spb = jnp.concatenate(sps, axis=0)
        half = ns * t
        cs = jnp.concatenate([_dot(spb[:half], uo), _dot(spb[half:], uo)], axis=0)
        return ds, cs

    def stage_output(pr, c, ds, cs, top, ns):
        cols = slice(pr * LANES, (pr + 1) * LANES)
        start = pl.multiple_of(c * kc, kc)
        v2 = v_ref[0, pl.ds(start, ns * LANES), cols]
        firstk = lax.broadcasted_iota(I32, (ns * LANES, LANES), 1) < HEAD_DIM
        zk = jnp.zeros((ns * LANES, LANES), BF16)
        ntile = 2 * t // rt
        runs = [jnp.zeros((rt, LANES), F32) if top else car_ref[pr, k * rt:(k + 1) * rt]
                for k in range(ntile)]
        pieces = [[None] * ntile for _ in range(ns)]
        for j in reversed(range(ns)):
            for k in range(ntile):
                r0 = j * 2 * t + k * rt
                a = jnp.exp(ds[r0 // rt] - (cs[r0:r0 + rt, :LANES] + runs[k]))
                if top and j == ns - 1:
                    a = jnp.where(causal_tile(r0), a, 0.0)
                pieces[j][k] = a.astype(BF16)
                runs[k] = runs[k] + cs[r0:r0 + rt, LANES:]
        for k in range(ntile):
            car_ref[pr, k * rt:(k + 1) * rt] = runs[k]
        pieces = [jnp.concatenate(p, axis=0) for p in pieces]
        a_e = jnp.concatenate([p[:t] for p in pieces], axis=1)
        a_o = jnp.concatenate([p[t:] for p in pieces], axis=1)
        pv = _dot(a_e, jnp.where(firstk, v2, zk)) + _dot(a_o, jnp.where(firstk, zk, v2))
        if top:
            o_ref[0, :, cols] = pv
        else:
            o_ref[0, :, cols] += pv

    def all_pairs(c, top, ns):
        us, mids = {}, {}
        for step in range(npair + 2):
            if step < npair:
                us[step] = stage_scores(step, c, ns)
            if 0 <= step - 1 < npair:
                mids[step - 1] = stage_cumsum(us.pop(step - 1), top, ns)
            if 0 <= step - 2 < npair:
                stage_output(step - 2, c, *mids.pop(step - 2), top, ns)

    for r in range(nsub):
        @pl.when(i % nsub == r)
        def _(r=r):
            all_pairs(ct, True, r + 1)

    def body(j, carry):
        all_pairs(ct - 1 - j, False, nsub)
        return carry

    lax.fori_loop(0, ct, body, 0)


def _sb_attention(q, k, v):
    bsz, s, w = q.shape
    npair = w // LANES
    nq = s // Q_BLOCK
    return pl.pallas_call(
        functools.partial(_sb_kernel, npair=npair),
        grid=(bsz, nq),
        in_specs=[pl.BlockSpec((1, Q_BLOCK, w), lambda b, i: (b, i, 0)),
                  pl.BlockSpec((1, s, w), lambda b, i: (b, 0, 0)),
                  pl.BlockSpec((1, s, w), lambda b, i: (b, 0, 0))],
        out_specs=pl.BlockSpec((1, Q_BLOCK, w), lambda b, i: (b, i, 0)),
        out_shape=jax.ShapeDtypeStruct((bsz, s, w), F32),
        scratch_shapes=[pltpu.VMEM((npair, 2 * Q_BLOCK, LANES), F32)],
        compiler_params=pltpu.CompilerParams(
            dimension_semantics=("parallel", "arbitrary"), vmem_limit_bytes=VMEM_LIMIT),
        name="sb_attention",
    )(q, k, v)


def _key_to_float(k):
    return lax.bitcast_convert_type(k ^ ((k >> 31) & 0x7FFFFFFF), F32)


def _dsa_kernel(dq_ref, kda_ref, vd1t_ref, qi2_ref, ki1_ref, wt_ref, o_ref,
                key_ref, qa_ref, acc_ref, m_ref, thr_ref, sc_ref, *, topk, n_heads):
    i = pl.program_id(1)
    t = Q_BLOCK
    kc = DSA_KC
    n_kc = i // (kc // t) + 1
    kf = float(topk)
    lane = lax.broadcasted_iota(I32, (t, LANES), 1)
    row = lax.broadcasted_iota(I32, (t, LANES), 0)
    first = lane < HEAD_DIM

    zb = jnp.zeros((t, LANES), BF16)
    for hd in range(n_heads):
        q2 = dq_ref[0, :, (hd // 2) * LANES:(hd // 2 + 1) * LANES]
        slope = 2.0 ** (-8.0 * (hd + 1) / n_heads)
        aug = jnp.where(lane == 0, LANES * slope, jnp.where(lane == 1, slope, 0.0)).astype(BF16)
        qa_ref[hd * t:(hd + 1) * t, 0:LANES] = (
            jnp.where(first, q2, zb) if hd % 2 == 0 else jnp.where(first, zb, q2))
        qa_ref[hd * t:(hd + 1) * t, LANES:2 * LANES] = aug

    wt = wt_ref[0]
    rowlane = (lax.broadcasted_iota(I32, (kc, LANES), 0)
               - lax.broadcasted_iota(I32, (kc, LANES), 1))

    def score_chunk(c, carry):
        start = pl.multiple_of(c * kc, kc)
        sc_ref[pl.ds(start, kc), :] = _nt(kda_ref[0, pl.ds(start, kc), :], qa_ref[...])
        k1 = ki1_ref[0, pl.ds(start, kc), :]
        lhs = jnp.concatenate([k1, k1], axis=1)
        score = None
        for pr in range(N_IDX_HEADS // 2):
            w2 = jnp.concatenate(
                [qi2_ref[0, :, 2 * hd * LANES:(2 * hd + 2) * LANES] for hd in (2 * pr, 2 * pr + 1)],
                axis=0)
            lg = _nt(lhs, w2)
            for half in range(2):
                hd = 2 * pr + half
                term = jnp.maximum(lg[:, half * LANES:(half + 1) * LANES], 0.0) * wt[hd:hd + 1, :]
                score = term if score is None else score + term
        key_ref[pl.ds(start, kc), :] = jnp.where(rowlane > i * t - c * kc, -jnp.inf, score)
        return carry

    lax.fori_loop(0, n_kc, score_chunk, 0)

    def count(pred):
        def body(c, acc):
            start = pl.multiple_of(c * kc, kc)
            m = jnp.where(pred(key_ref[pl.ds(start, kc), :]), 1.0, 0.0)
            m = m.reshape(8, kc // 8, LANES).sum(axis=0)
            return acc + m.reshape(kc // 64, 8, LANES).sum(axis=0)
        part = lax.fori_loop(0, n_kc, body, jnp.zeros((8, LANES), F32))
        return jnp.sum(part, axis=0, keepdims=True)

    searching = (i + 1) * t > topk

    def search(nk):
        def bit_step(bi, lo):
            cand = lo + lax.shift_left(jnp.int32(1), 31 - bi)
            cand_f = _key_to_float(cand)
            acc = None
            for c in range(nk):
                m = jnp.where(key_ref[c * kc:(c + 1) * kc, :] >= cand_f, 1.0, 0.0)
                m = m.reshape(8, kc // 8, LANES).sum(axis=0)
                acc = m if acc is None else acc + m
            cnt = jnp.sum(acc.reshape(kc // 64, 8, LANES).sum(axis=0), axis=0, keepdims=True)
            return jnp.where(cnt >= kf, cand, lo)

        return _key_to_float(lax.fori_loop(0, 32, bit_step, jnp.full((1, LANES), INT_MIN, I32)))

    thr_ref[...] = jnp.full(thr_ref.shape, jnp.finfo(F32).min, F32)
    for nk in range(1, key_ref.shape[0] // kc + 1):
        @pl.when(searching & (n_kc == nk))
        def _(nk=nk):
            thr_ref[...] = jnp.broadcast_to(search(nk), thr_ref.shape)

    thr = thr_ref[0:1, :]
    cnt_ge = count(lambda blk: blk >= thr)

    @pl.when((jnp.max(cnt_ge) > kf) & searching)
    def _():
        need = kf - count(lambda blk: blk > thr)
        sl = jnp.where(lane < row, 1.0, 0.0).astype(BF16)
        nsub = kc // t

        def body(c, before):
            start = pl.multiple_of(c * kc, kc)
            subs = [key_ref[pl.ds(start + j * t, t), :] for j in range(nsub)]
            eqfs = [jnp.where(sub == thr, 1.0, 0.0) for sub in subs]
            ranks = [_dot(sl, eqf.astype(BF16)) for eqf in eqfs]
            for j in range(nsub):
                demote = (subs[j] == thr) & (ranks[j] + before >= need)
                key_ref[pl.ds(start + j * t, t), :] = jnp.where(demote, -jnp.inf, subs[j])
                before = before + jnp.sum(eqfs[j], axis=0, keepdims=True)
            return before

        lax.fori_loop(0, n_kc, body, jnp.zeros((1, LANES), F32))

    m_ref[...] = jnp.full(m_ref.shape, NEG_BIG, F32)
    acc_ref[...] = jnp.zeros(acc_ref.shape, F32)

    def attn_chunk(c, carry):
        start = pl.multiple_of(c * kc, kc)
        vt = vd1t_ref[0, :, pl.ds(start, kc)]
        bias = jnp.where(key_ref[pl.ds(start, kc), :] >= thr, 0.0, -jnp.inf)
        sc = sc_ref[pl.ds(start, kc), :] + jnp.concatenate([bias] * n_heads, axis=1)
        m_old = m_ref[...]
        m_new = jnp.maximum(m_old, jnp.max(sc, axis=0, keepdims=True))
        h = kc // 2
        pv = (_dot(vt[:, :h], jnp.exp(sc[:h] - m_new).astype(BF16))
              + _dot(vt[:, h:], jnp.exp(sc[h:] - m_new).astype(BF16)))
        acc_ref[...] = acc_ref[...] * jnp.exp(m_old - m_new) + pv
        m_ref[...] = m_new
        return carry

    lax.fori_loop(0, n_kc, attn_chunk, 0)

    for pr in range(n_heads // 2):
        a_e = acc_ref[:, (2 * pr) * t:(2 * pr + 1) * t]
        a_o = acc_ref[:, (2 * pr + 1) * t:(2 * pr + 2) * t]
        r = jnp.concatenate([a_e[:HEAD_DIM] / a_e[HEAD_DIM:], a_o[:HEAD_DIM] / a_o[HEAD_DIM:]], axis=0)
        o_ref[0, :, pr * LANES:(pr + 1) * LANES] = r.T


def _dsa_attention(dq, kda, vd1t, qi2, ki1, wt, *, topk):
    bsz, s, w = dq.shape
    nq = s // Q_BLOCK
    n_heads = w // HEAD_DIM
    blk = lambda wd: pl.BlockSpec((1, Q_BLOCK, wd), lambda b, i: (b, i, 0))
    full = lambda wd: pl.BlockSpec((1, s, wd), lambda b, i: (b, 0, 0))
    return pl.pallas_call(
        functools.partial(_dsa_kernel, topk=topk, n_heads=n_heads),
        grid=(bsz, nq),
        in_specs=[blk(w), full(2 * LANES),
                  pl.BlockSpec((1, LANES, s), lambda b, i: (b, 0, 0)),
                  blk(qi2.shape[2]), full(LANES),
                  pl.BlockSpec((1, 8, Q_BLOCK), lambda b, i: (b, 0, i))],
        out_specs=blk(w),
        out_shape=jax.ShapeDtypeStruct((bsz, s, w), F32),
        scratch_shapes=[pltpu.VMEM((s, LANES), F32),
                        pltpu.VMEM((n_heads * Q_BLOCK, 2 * LANES), BF16),
                        pltpu.VMEM((LANES, n_heads * Q_BLOCK), F32),
                        pltpu.VMEM((1, n_heads * Q_BLOCK), F32),
                        pltpu.VMEM((8, LANES), F32),
                        pltpu.VMEM((s, n_heads * Q_BLOCK), F32)],
        compiler_params=pltpu.CompilerParams(
            dimension_semantics=("parallel", "arbitrary"), vmem_limit_bytes=VMEM_LIMIT),
        name="dsa_attention",
    )(dq, kda, vd1t, qi2, ki1, wt)


def _out_kernel(x_ref, osb_ref, odsa_ref, mod_ref, gsb_ref, gdsa_ref, wo_ref, o_ref, *, sb_w):
    a = _rms(osb_ref[0], gsb_ref[...]).astype(BF16)
    b = _rms(odsa_ref[0], gdsa_ref[...]).astype(BF16)
    y = _dot(a, wo_ref[0:sb_w, :]) + _dot(b, wo_ref[sb_w:, :])
    o_ref[0] = x_ref[0] + mod_ref[0][2:3] * y


def _out_proj(x, osb, odsa, mod, gsb, gdsa, wo, tm=512):
    bsz, s, d = x.shape
    sb_w, dsa_w = osb.shape[2], odsa.shape[2]
    row = lambda w: pl.BlockSpec((1, tm, w), lambda b, i: (b, i, 0))
    full = lambda shape: pl.BlockSpec(shape, lambda b, i: (0,) * len(shape))
    return pl.pallas_call(
        functools.partial(_out_kernel, sb_w=sb_w),
        grid=(bsz, s // tm),
        in_specs=[row(d), row(sb_w), row(dsa_w),
                  pl.BlockSpec((1, 6, d), lambda b, i: (b, 0, 0)),
                  full((1, sb_w)), full((1, dsa_w)), full((sb_w + dsa_w, d))],
        out_specs=row(d),
        out_shape=jax.ShapeDtypeStruct((bsz, s, d), F32),
        compiler_params=pltpu.CompilerParams(
            dimension_semantics=("parallel", "parallel"), vmem_limit_bytes=VMEM_LIMIT),
        name="out_proj",
    )(x, osb, odsa, mod, gsb, gdsa, wo)


def _ffn_kernel(x_ref, mod_ref, g_ref, wg_ref, wu_ref, wd_ref, fg_ref, fmod_ref, o_ref,
                h_ref, acc_ref, *, final):
    j = pl.program_id(2)

    @pl.when(j == 0)
    def _():
        mod = mod_ref[0]
        h = _rms(x_ref[0], g_ref[...]) * (1.0 + mod[4:5]) + mod[3:4]
        h_ref[...] = h.astype(BF16)

    h = h_ref[...]
    g = _dot(h, wg_ref[...])
    u = _dot(h, wu_ref[...])
    act = (g / (1.0 + jnp.exp(-g))) * u
    part = _dot(act.astype(BF16), wd_ref[...])

    @pl.when(j == 0)
    def _():
        acc_ref[...] = part

    @pl.when(j > 0)
    def _():
        acc_ref[...] += part

    @pl.when(j == pl.num_programs(2) - 1)
    def _():
        y = x_ref[0] + mod_ref[0][5:6] * acc_ref[...]
        if final:
            fmod = fmod_ref[0]
            y = _rms(y, fg_ref[...]) * (1.0 + fmod[1:2]) + fmod[0:1]
        o_ref[0] = y


def _ffn(x, mod, gain, wg, wu, wd, fgain, fmod, *, final, tm=512, tf=1408):
    bsz, s, d = x.shape
    dff = wg.shape[1]
    tm = min(tm, s)
    row = pl.BlockSpec((1, tm, d), lambda b, i, j: (b, i, 0))
    return pl.pallas_call(
        functools.partial(_ffn_kernel, final=final),
        grid=(bsz, s // tm, dff // tf),
        in_specs=[row,
                  pl.BlockSpec((1, 6, d), lambda b, i, j: (b, 0, 0)),
                  pl.BlockSpec((1, d), lambda b, i, j: (0, 0)),
                  pl.BlockSpec((d, tf), lambda b, i, j: (0, j)),
                  pl.BlockSpec((d, tf), lambda b, i, j: (0, j)),
                  pl.BlockSpec((tf, d), lambda b, i, j: (j, 0)),
                  pl.BlockSpec((1, d), lambda b, i, j: (0, 0)),
                  pl.BlockSpec((1, 2, d), lambda b, i, j: (b, 0, 0))],
        out_specs=row,
        out_shape=jax.ShapeDtypeStruct((bsz, s, d), F32),
        scratch_shapes=[pltpu.VMEM((tm, d), BF16), pltpu.VMEM((tm, d), F32)],
        compiler_params=pltpu.CompilerParams(
            dimension_semantics=("parallel", "parallel", "arbitrary"),
            vmem_limit_bytes=VMEM_LIMIT),
        name="swiglu_ffn",
    )(x, mod, gain, wg, wu, wd, fgain, fmod)


def kernel(x, c, w_mod, b_mod, norm1_gain, norm2_gain, w_in, kv_gain, w_uk, w_uv, sb_out_gain,
           dsa_out_gain, w_o, w_gate, w_up, w_down, w_mod_final, b_mod_final, final_gain):
    bsz, s, d = x.shape
    depth = w_in.shape[0]
    sb_w = sb_out_gain.shape[1]
    dsa_w = dsa_out_gain.shape[1]
    kv_w = kv_gain.shape[1]
    topk = min(TOPK_MAX, s // 4)
    assert s % max(SB_KC, DSA_KC) == 0 and topk % Q_BLOCK == 0
    assert HEAD_DIM * 2 == LANES and w_uk.shape[2] == HEAD_DIM and s // LANES <= 256

    mod = _mod(c, w_mod, b_mod).reshape(depth, bsz, 6, d)
    fmod = _mod(c, w_mod_final[None], b_mod_final[None]).reshape(bsz, 2, d)

    n_a = 3 * sb_w + dsa_w + kv_w
    n_idx = N_IDX_HEADS * IDX_DIM + IDX_DIM + N_IDX_HEADS
    idx_pad = N_IDX_HEADS * IDX_DIM + LANES - n_idx
    bkv = jnp.concatenate([jnp.zeros((3 * HEAD_DIM,), F32), jnp.ones((HEAD_DIM,), F32)])[None]

    for l in range(depth):
        wa = w_in[l][:, :n_a].astype(BF16)
        wi = jnp.pad(w_in[l][:, n_a:], ((0, 0), (0, idx_pad)))
        wih = wi.astype(BF16)
        wil = (wi - wih.astype(F32)).astype(BF16)
        wkv = jnp.concatenate([w_uk[l], w_uk[l], w_uv[l], jnp.zeros_like(w_uv[l])],
                              axis=1).astype(BF16)
        (sbq, sbk, sbv, dq, kda, qi2, ki1, wt, vd1t) = _in_proj(
            x, mod[l], norm1_gain[l][None], wa, wih, wil, kv_gain[l][None], wkv, bkv,
            sb_w=sb_w, dsa_w=dsa_w, kv_w=kv_w)
        osb = _sb_attention(sbq, sbk, sbv)
        odsa = _dsa_attention(dq, kda, vd1t, qi2, ki1, wt, topk=topk)
        x = _out_proj(x, osb, odsa, mod[l], sb_out_gain[l][None], dsa_out_gain[l][None],
                      w_o[l].astype(BF16))
        x = _ffn(x, mod[l], norm2_gain[l][None], w_gate[l].astype(BF16), w_up[l].astype(BF16),
                 w_down[l].astype(BF16), final_gain[None], fmod, final=(l == depth - 1))
    return x
```

```python
import functools

import jax
import jax.numpy as jnp
from jax import lax
from jax.experimental import pallas as pl
from jax.experimental.pallas import tpu as pltpu

F32 = jnp.float32
BF16 = jnp.bfloat16
I32 = jnp.int32

HEAD_DIM = 64
LANES = 128
Q_BLOCK = 128
N_IDX_HEADS = 4
IDX_DIM = 64
TOPK_MAX = 256
RMS_EPS = 1e-6
INT_MIN = -2 ** 31
NEG_BIG = -1e30
VMEM_LIMIT = 52 * 1024 * 1024
SB_KC = 512
SB_ROW_TILE = 64
DSA_KC = 512


def _nt(a, b):
    return lax.dot_general(a, b, (((1,), (1,)), ((), ())), preferred_element_type=F32)


def _dot(a, b):
    return jnp.dot(a, b, preferred_element_type=F32)


def _split(x):
    hi = x.astype(BF16)
    lo = (x - hi.astype(F32)).astype(BF16)
    return hi, lo


def _rms(x, gain):
    ms = jnp.mean(x * x, axis=-1, keepdims=True)
    return x * lax.rsqrt(ms + RMS_EPS) * gain


def _mod_kernel(c_ref, w_ref, b_ref, o_ref):
    c = c_ref[...]
    ca = c / (1.0 + jnp.exp(-c))
    ch, cl = _split(ca)
    wh, wl = _split(w_ref[0])
    o_ref[0] = _dot(ch, wh) + _dot(ch, wl) + _dot(cl, wh) + b_ref[0]


def _mod(c, w, b, tn=1024):
    nl, d, n = w.shape
    bsz = c.shape[0]
    return pl.pallas_call(
        _mod_kernel,
        grid=(nl, n // tn),
        in_specs=[pl.BlockSpec((bsz, d), lambda l, j: (0, 0)),
                  pl.BlockSpec((1, d, tn), lambda l, j: (l, 0, j)),
                  pl.BlockSpec((1, 1, tn), lambda l, j: (l, 0, j))],
        out_specs=pl.BlockSpec((1, bsz, tn), lambda l, j: (l, 0, j)),
        out_shape=jax.ShapeDtypeStruct((nl, bsz, n), F32),
        name="adaln_mod",
    )(c, w, b.reshape(nl, 1, n))


def _in_kernel(x_ref, mod_ref, g_ref, wa_ref, wih_ref, wil_ref, kvg_ref, wkv_ref, bkv_ref,
               sbq_ref, sbk_ref, sbv_ref, dq_ref, kda_ref, qi2_ref, ki1_ref, wt_ref, vd1t_ref,
               *, sb_w, dsa_w, kv_w):
    tm = x_ref.shape[1]
    x = x_ref[0]
    mod = mod_ref[0]
    h = _rms(x, g_ref[...]) * (1.0 + mod[1:2]) + mod[0:1]
    hh, hl = _split(h)
    main = _dot(hh, wa_ref[...])
    scale = HEAD_DIM ** -0.5
    sbq_ref[0] = (main[:, 0:sb_w] * scale).astype(BF16)
    sbk_ref[0] = main[:, sb_w:2 * sb_w].astype(BF16)
    sbv_ref[0] = main[:, 2 * sb_w:3 * sb_w].astype(BF16)
    o = 3 * sb_w
    dq_ref[0] = (main[:, o:o + dsa_w] * scale).astype(BF16)
    o += dsa_w
    kv = _rms(main[:, o:o + kv_w], kvg_ref[...])
    kvp = _dot(kv.astype(BF16), wkv_ref[...]) + bkv_ref[...]
    lane = lax.broadcasted_iota(I32, (tm, LANES), 1)
    pos = pl.program_id(1) * tm + lax.broadcasted_iota(I32, (tm, LANES), 0)
    posf = jnp.where(lane == 0, pos >> 7, jnp.where(lane == 1, pos & (LANES - 1), 0)).astype(F32)
    kda_ref[0, :, 0:LANES] = kvp[:, 0:LANES].astype(BF16)
    kda_ref[0, :, LANES:2 * LANES] = posf.astype(BF16)
    vd1t_ref[0] = kvp[:, LANES:2 * LANES].T.astype(BF16)
    wih = wih_ref[...]
    idx = _dot(hh, wih) + _dot(hh, wil_ref[...]) + _dot(hl, wih)
    first = lane < IDX_DIM
    for pr in range(N_IDX_HEADS // 2):
        pair = idx[:, pr * LANES:(pr + 1) * LANES] * (IDX_DIM ** -0.5)
        rolled = pltpu.roll(pair, IDX_DIM, 1)
        for half in range(2):
            dup = jnp.where(first, pair, rolled) if half == 0 else jnp.where(first, rolled, pair)
            qh, ql = _split(dup)
            hd = 2 * pr + half
            qi2_ref[0, :, 2 * hd * LANES:(2 * hd + 1) * LANES] = qh
            qi2_ref[0, :, (2 * hd + 1) * LANES:(2 * hd + 2) * LANES] = jnp.where(
                first, ql, jnp.zeros_like(ql))
    nq = N_IDX_HEADS * IDX_DIM
    kw = idx[:, nq:nq + LANES]
    kh, kl = _split(jnp.where(first, kw, pltpu.roll(kw, IDX_DIM, 1)))
    ki1_ref[0] = jnp.where(first, kh, kl)
    kwt = kw.T
    wt_ref[0] = kwt[IDX_DIM:IDX_DIM + 8, :] * (N_IDX_HEADS ** -0.5)


def _in_proj(x, mod, gain, wa, wih, wil, kvg, wkv, bkv, *, sb_w, dsa_w, kv_w, tm=512):
    bsz, s, d = x.shape
    na = wa.shape[1]
    ni = wih.shape[1]
    nkv = wkv.shape[1]
    full = lambda shape: pl.BlockSpec(shape, lambda b, i: (0,) * len(shape))
    row = lambda w: pl.BlockSpec((1, tm, w), lambda b, i: (b, i, 0))
    outs = [
        (sb_w, BF16), (sb_w, BF16), (sb_w, BF16), (dsa_w, BF16),
        (2 * LANES, BF16), (2 * LANES * N_IDX_HEADS, BF16), (LANES, BF16),
    ]
    out_shape = [jax.ShapeDtypeStruct((bsz, s, w), dt) for w, dt in outs]
    out_specs = [row(w) for w, _ in outs]
    for rows, dt in ((8, F32), (LANES, BF16)):
        out_shape.append(jax.ShapeDtypeStruct((bsz, rows, s), dt))
        out_specs.append(pl.BlockSpec((1, rows, tm), lambda b, i: (b, 0, i)))
    return pl.pallas_call(
        functools.partial(_in_kernel, sb_w=sb_w, dsa_w=dsa_w, kv_w=kv_w),
        grid=(bsz, s // tm),
        in_specs=[row(d),
                  pl.BlockSpec((1, 6, d), lambda b, i: (b, 0, 0)),
                  full((1, d)), full((d, na)), full((d, ni)), full((d, ni)),
                  full((1, kv_w)), full((kv_w, nkv)), full((1, nkv))],
        out_specs=out_specs,
        out_shape=out_shape,
        compiler_params=pltpu.CompilerParams(
            dimension_semantics=("parallel", "parallel"), vmem_limit_bytes=VMEM_LIMIT),
        name="in_proj",
    )(x, mod, gain, wa, wih, wil, kvg, wkv, bkv)


def _sb_kernel(q_ref, k_ref, v_ref, o_ref, car_ref, *, npair):
    i = pl.program_id(1)
    t = Q_BLOCK
    kc = SB_KC
    nsub = kc // LANES
    lane = lax.broadcasted_iota(I32, (t, LANES), 1)
    first = lane < HEAD_DIM
    zb = jnp.zeros((t, LANES), BF16)
    r2 = lax.broadcasted_iota(I32, (LANES, 2 * LANES), 0)
    c2 = lax.broadcasted_iota(I32, (LANES, 2 * LANES), 1)
    uo = jnp.where((c2 >= LANES) | (r2 > c2), 1.0, 0.0).astype(BF16)
    ct = i // nsub
    rt = SB_ROW_TILE

    def stage_scores(pr, c, ns):
        cols = slice(pr * LANES, (pr + 1) * LANES)
        start = pl.multiple_of(c * kc, kc)
        q2 = q_ref[0, :, cols]
        qs = jnp.concatenate([jnp.where(first, q2, zb), jnp.where(first, zb, q2)], axis=0)
        z = _nt(qs, k_ref[0, pl.ds(start, ns * LANES), cols])
        return jnp.concatenate([z[:, j * LANES:(j + 1) * LANES] for j in range(ns)], axis=0)

    def causal_tile(r0):
        rr = (r0 + lax.broadcasted_iota(I32, (rt, LANES), 0)) & (t - 1)
        return lax.broadcasted_iota(I32, (rt, LANES), 1) < rr

    def stage_cumsum(u, top, ns):
        ds, sps = [], []
        for r0 in range(0, ns * 2 * t, rt):
            ut = u[r0:r0 + rt]
            sp = jnp.maximum(ut, 0.0) + jnp.log(1.0 + jnp.exp(-jnp.abs(ut)))
            ds.append(ut - sp)
            if top and r0 >= (ns - 1) * 2 * t:
                sp = jnp.where(causal_tile(r0), sp, 0.0)
            sps.append(sp.astype(BF16))
        spb = jnp.concatenate(sps, axis=0)
        half = ns * t
        cs = jnp.concatenate([_dot(spb[:half], uo), _dot(spb[half:], uo)], axis=0)
        return ds, cs

    def stage_output(pr, c, ds, cs, top, ns):
        cols = slice(pr * LANES, (pr + 1) * LANES)
        start = pl.multiple_of(c * kc, kc)
        v2 = v_ref[0, pl.ds(start, ns * LANES), cols]
        firstk = lax.broadcasted_iota(I32, (ns * LANES, LANES), 1) < HEAD_DIM
        zk = jnp.zeros((ns * LANES, LANES), BF16)
        ntile = 2 * t // rt
        runs = [jnp.zeros((rt, LANES), F32) if top else car_ref[pr, k * rt:(k + 1) * rt]
                for k in range(ntile)]
        pieces = [[None] * ntile for _ in range(ns)]
        for j in reversed(range(ns)):
            for k in range(ntile):
                r0 = j * 2 * t + k * rt
                a = jnp.exp(ds[r0 // rt] - (cs[r0:r0 + rt, :LANES] + runs[k]))
                if top and j == ns - 1:
                    a = jnp.where(causal_tile(r0), a, 0.0)
                pieces[j][k] = a.astype(BF16)
                runs[k] = runs[k] + cs[r0:r0 + rt, LANES:]
        for k in range(ntile):
            car_ref[pr, k * rt:(k + 1) * rt] = runs[k]
        pieces = [jnp.concatenate(p, axis=0) for p in pieces]
        a_e = jnp.concatenate([p[:t] for p in pieces], axis=1)
        a_o = jnp.concatenate([p[t:] for p in pieces], axis=1)
        pv = _dot(a_e, jnp.where(firstk, v2, zk)) + _dot(a_o, jnp.where(firstk, zk, v2))
        if top:
            o_ref[0, :, cols] = pv
        else:
            o_ref[0, :, cols] += pv

    def all_pairs(c, top, ns):
        us, mids = {}, {}
        for step in range(npair + 2):
            if step < npair:
                us[step] = stage_scores(step, c, ns)
            if 0 <= step - 1 < npair:
                mids[step - 1] = stage_cumsum(us.pop(step - 1), top, ns)
            if 0 <= step - 2 < npair:
                stage_output(step - 2, c, *mids.pop(step - 2), top, ns)

    for r in range(nsub):
        @pl.when(i % nsub == r)
        def _(r=r):
            all_pairs(ct, True, r + 1)

    def body(j, carry):
        all_pairs(ct - 1 - j, False, nsub)
        return carry

    lax.fori_loop(0, ct, body, 0)


def _sb_attention(q, k, v):
    bsz, s, w = q.shape
    npair = w // LANES
    nq = s // Q_BLOCK
    return pl.pallas_call(
        functools.partial(_sb_kernel, npair=npair),
        grid=(bsz, nq),
        in_specs=[pl.BlockSpec((1, Q_BLOCK, w), lambda b, i: (b, i, 0)),
                  pl.BlockSpec((1, s, w), lambda b, i: (b, 0, 0)),
                  pl.BlockSpec((1, s, w), lambda b, i: (b, 0, 0))],
        out_specs=pl.BlockSpec((1, Q_BLOCK, w), lambda b, i: (b, i, 0)),
        out_shape=jax.ShapeDtypeStruct((bsz, s, w), F32),
        scratch_shapes=[pltpu.VMEM((npair, 2 * Q_BLOCK, LANES), F32)],
        compiler_params=pltpu.CompilerParams(
            dimension_semantics=("parallel", "arbitrary"), vmem_limit_bytes=VMEM_LIMIT),
        name="sb_attention",
    )(q, k, v)


def _key_to_float(k):
    return lax.bitcast_convert_type(k ^ ((k >> 31) & 0x7FFFFFFF), F32)


def _dsa_kernel(dq_ref, kda_ref, vd1t_ref, qi2_ref, ki1_ref, wt_ref, o_ref,
                key_ref, qa_ref, acc_ref, m_ref, thr_ref, sc_ref, *, topk, n_heads):
    i = pl.program_id(1)
    t = Q_BLOCK
    kc = DSA_KC
    n_kc = i // (kc // t) + 1
    kf = float(topk)
    lane = lax.broadcasted_iota(I32, (t, LANES), 1)
    row = lax.broadcasted_iota(I32, (t, LANES), 0)
    first = lane < HEAD_DIM

    zb = jnp.zeros((t, LANES), BF16)
    for hd in range(n_heads):
        q2 = dq_ref[0, :, (hd // 2) * LANES:(hd // 2 + 1) * LANES]
        slope = 2.0 ** (-8.0 * (hd + 1) / n_heads)
        aug = jnp.where(lane == 0, LANES * slope, jnp.where(lane == 1, slope, 0.0)).astype(BF16)
        qa_ref[hd * t:(hd + 1) * t, 0:LANES] = (
            jnp.where(first, q2, zb) if hd % 2 == 0 else jnp.where(first, zb, q2))
        qa_ref[hd * t:(hd + 1) * t, LANES:2 * LANES] = aug

    wt = wt_ref[0]
    rowlane = (lax.broadcasted_iota(I32, (kc, LANES), 0)
               - lax.broadcasted_iota(I32, (kc, LANES), 1))

    def score_chunk(c, carry):
        start = pl.multiple_of(c * kc, kc)
        sc_ref[pl.ds(start, kc), :] = _nt(kda_ref[0, pl.ds(start, kc), :], qa_ref[...])
        k1 = ki1_ref[0, pl.ds(start, kc), :]
        lhs = jnp.concatenate([k1, k1], axis=1)
        score = None
        for pr in range(N_IDX_HEADS // 2):
            w2 = jnp.concatenate(
                [qi2_ref[0, :, 2 * hd * LANES:(2 * hd + 2) * LANES] for hd in (2 * pr, 2 * pr + 1)],
                axis=0)
            lg = _nt(lhs, w2)
            for half in range(2):
                hd = 2 * pr + half
                term = jnp.maximum(lg[:, half * LANES:(half + 1) * LANES], 0.0) * wt[hd:hd + 1, :]
                score = term if score is None else score + term
        key_ref[pl.ds(start, kc), :] = jnp.where(rowlane > i * t - c * kc, -jnp.inf, score)
        return carry

    lax.fori_loop(0, n_kc, score_chunk, 0)

    def count(pred):
        def body(c, acc):
            start = pl.multiple_of(c * kc, kc)
            m = jnp.where(pred(key_ref[pl.ds(start, kc), :]), 1.0, 0.0)
            m = m.reshape(8, kc // 8, LANES).sum(axis=0)
            return acc + m.reshape(kc // 64, 8, LANES).sum(axis=0)
        part = lax.fori_loop(0, n_kc, body, jnp.zeros((8, LANES), F32))
        return jnp.sum(part, axis=0, keepdims=True)

    searching = (i + 1) * t > topk

    def search(nk):
        def bit_step(bi, lo):
            cand = lo + lax.shift_left(jnp.int32(1), 31 - bi)
            cand_f = _key_to_float(cand)
            acc = None
            for c in range(nk):
                m = jnp.where(key_ref[c * kc:(c + 1) * kc, :] >= cand_f, 1.0, 0.0)
                m = m.reshape(8, kc // 8, LANES).sum(axis=0)
                acc = m if acc is None else acc + m
            cnt = jnp.sum(acc.reshape(kc // 64, 8, LANES).sum(axis=0), axis=0, keepdims=True)
            return jnp.where(cnt >= kf, cand, lo)

        return _key_to_float(lax.fori_loop(0, 32, bit_step, jnp.full((1, LANES), INT_MIN, I32)))

    thr_ref[...] = jnp.full(thr_ref.shape, jnp.finfo(F32).min, F32)
    for nk in range(1, key_ref.shape[0] // kc + 1):
        @pl.when(searching & (n_kc == nk))
        def _(nk=nk):
            thr_ref[...] = jnp.broadcast_to(search(nk), thr_ref.shape)

    thr = thr_ref[0:1, :]
    cnt_ge = count(lambda blk: blk >= thr)

    @pl.when((jnp.max(cnt_ge) > kf) & searching)
    def _():
        need = kf - count(lambda blk: blk > thr)
        sl = jnp.where(lane < row, 1.0, 0.0).astype(BF16)
        nsub = kc // t

        def body(c, before):
            start = pl.multiple_of(c * kc, kc)
            subs = [key_ref[pl.ds(start + j * t, t), :] for j in range(nsub)]
            eqfs = [jnp.where(sub == thr, 1.0, 0.0) for sub in subs]
            ranks = [_dot(sl, eqf.astype(BF16)) for eqf in eqfs]
            for j in range(nsub):
                demote = (subs[j] == thr) & (ranks[j] + before >= need)
                key_ref[pl.ds(start + j * t, t), :] = jnp.where(demote, -jnp.inf, subs[j])
                before = before + jnp.sum(eqfs[j], axis=0, keepdims=True)
            return before

        lax.fori_loop(0, n_kc, body, jnp.zeros((1, LANES), F32))

    m_ref[...] = jnp.full(m_ref.shape, NEG_BIG, F32)
    acc_ref[...] = jnp.zeros(acc_ref.shape, F32)

    def attn_chunk(c, carry):
        start = pl.multiple_of(c * kc, kc)
        vt = vd1t_ref[0, :, pl.ds(start, kc)]
        bias = jnp.where(key_ref[pl.ds(start, kc), :] >= thr, 0.0, -jnp.inf)
        sc = sc_ref[pl.ds(start, kc), :] + jnp.concatenate([bias] * n_heads, axis=1)
        m_old = m_ref[...]
        m_new = jnp.maximum(m_old, jnp.max(sc, axis=0, keepdims=True))
        h = kc // 2
        pv = (_dot(vt[:, :h], jnp.exp(sc[:h] - m_new).astype(BF16))
              + _dot(vt[:, h:], jnp.exp(sc[h:] - m_new).astype(BF16)))
        acc_ref[...] = acc_ref[...] * jnp.exp(m_old - m_new) + pv
        m_ref[...] = m_new
        return carry

    lax.fori_loop(0, n_kc, attn_chunk, 0)

    for pr in range(n_heads // 2):
        a_e = acc_ref[:, (2 * pr) * t:(2 * pr + 1) * t]
        a_o = acc_ref[:, (2 * pr + 1) * t:(2 * pr + 2) * t]
        r = jnp.concatenate([a_e[:HEAD_DIM] / a_e[HEAD_DIM:], a_o[:HEAD_DIM] / a_o[HEAD_DIM:]], axis=0)
        o_ref[0, :, pr * LANES:(pr + 1) * LANES] = r.T


def _dsa_attention(dq, kda, vd1t, qi2, ki1, wt, *, topk):
    bsz, s, w = dq.shape
    nq = s // Q_BLOCK
    n_heads = w // HEAD_DIM
    blk = lambda wd: pl.BlockSpec((1, Q_BLOCK, wd), lambda b, i: (b, i, 0))
    full = lambda wd: pl.BlockSpec((1, s, wd), lambda b, i: (b, 0, 0))
    return pl.pallas_call(
        functools.partial(_dsa_kernel, topk=topk, n_heads=n_heads),
        grid=(bsz, nq),
        in_specs=[blk(w), full(2 * LANES),
                  pl.BlockSpec((1, LANES, s), lambda b, i: (b, 0, 0)),
                  blk(qi2.shape[2]), full(LANES),
                  pl.BlockSpec((1, 8, Q_BLOCK), lambda b, i: (b, 0, i))],
        out_specs=blk(w),
        out_shape=jax.ShapeDtypeStruct((bsz, s, w), F32),
        scratch_shapes=[pltpu.VMEM((s, LANES), F32),
                        pltpu.VMEM((n_heads * Q_BLOCK, 2 * LANES), BF16),
                        pltpu.VMEM((LANES, n_heads * Q_BLOCK), F32),
                        pltpu.VMEM((1, n_heads * Q_BLOCK), F32),
                        pltpu.VMEM((8, LANES), F32),
                        pltpu.VMEM((s, n_heads * Q_BLOCK), F32)],
        compiler_params=pltpu.CompilerParams(
            dimension_semantics=("parallel", "arbitrary"), vmem_limit_bytes=VMEM_LIMIT),
        name="dsa_attention",
    )(dq, kda, vd1t, qi2, ki1, wt)


def _ffn_kernel(x_ref, osb_ref, odsa_ref, mod_ref, gsb_ref, gdsa_ref, wo_ref, g_ref,
                wg_ref, wu_ref, wd_ref, fg_ref, fmod_ref, o_ref,
                x1_ref, h_ref, acc_ref, *, final):
    j = pl.program_id(2)

    @pl.when(j == 0)
    def _():
        mod = mod_ref[0]
        sb_w = osb_ref.shape[2]
        a = _rms(osb_ref[0], gsb_ref[...]).astype(BF16)
        b = _rms(odsa_ref[0], gdsa_ref[...]).astype(BF16)
        y = _dot(a, wo_ref[0:sb_w, :]) + _dot(b, wo_ref[sb_w:, :])
        x1 = x_ref[0] + mod[2:3] * y
        x1_ref[...] = x1
        h = _rms(x1, g_ref[...]) * (1.0 + mod[4:5]) + mod[3:4]
        h_ref[...] = h.astype(BF16)

    h = h_ref[...]
    g = _dot(h, wg_ref[...])
    u = _dot(h, wu_ref[...])
    act = (g / (1.0 + jnp.exp(-g))) * u
    part = _dot(act.astype(BF16), wd_ref[...])

    @pl.when(j == 0)
    def _():
        acc_ref[...] = part

    @pl.when(j > 0)
    def _():
        acc_ref[...] += part

    @pl.when(j == pl.num_programs(2) - 1)
    def _():
        y = x1_ref[...] + mod_ref[0][5:6] * acc_ref[...]
        if final:
            fmod = fmod_ref[0]
            y = _rms(y, fg_ref[...]) * (1.0 + fmod[1:2]) + fmod[0:1]
        o_ref[0] = y


def _out_ffn(x, osb, odsa, mod, gsb, gdsa, wo, gain, wg, wu, wd, fgain, fmod, *, final,
             tm=512, tf=1408):
    bsz, s, d = x.shape
    dff = wg.shape[1]
    sb_w, dsa_w = osb.shape[2], odsa.shape[2]
    tm = min(tm, s)
    row = lambda w: pl.BlockSpec((1, tm, w), lambda b, i, j: (b, i, 0))
    full = lambda shape: pl.BlockSpec(shape, lambda b, i, j: (0,) * len(shape))
    return pl.pallas_call(
        functools.partial(_ffn_kernel, final=final),
        grid=(bsz, s // tm, dff // tf),
        in_specs=[row(d), row(sb_w), row(dsa_w),
                  pl.BlockSpec((1, 6, d), lambda b, i, j: (b, 0, 0)),
                  full((1, sb_w)), full((1, dsa_w)), full((sb_w + dsa_w, d)), full((1, d)),
                  pl.BlockSpec((d, tf), lambda b, i, j: (0, j)),
                  pl.BlockSpec((d, tf), lambda b, i, j: (0, j)),
                  pl.BlockSpec((tf, d), lambda b, i, j: (j, 0)),
                  full((1, d)),
                  pl.BlockSpec((1, 2, d), lambda b, i, j: (b, 0, 0))],
        out_specs=row(d),
        out_shape=jax.ShapeDtypeStruct((bsz, s, d), F32),
        scratch_shapes=[pltpu.VMEM((tm, d), F32), pltpu.VMEM((tm, d), BF16),
                        pltpu.VMEM((tm, d), F32)],
        compiler_params=pltpu.CompilerParams(
            dimension_semantics=("parallel", "parallel", "arbitrary"),
            vmem_limit_bytes=VMEM_LIMIT),
        name="out_ffn",
    )(x, osb, odsa, mod, gsb, gdsa, wo, gain, wg, wu, wd, fgain, fmod)


def kernel(x, c, w_mod, b_mod, norm1_gain, norm2_gain, w_in, kv_gain, w_uk, w_uv, sb_out_gain,
           dsa_out_gain, w_o, w_gate, w_up, w_down, w_mod_final, b_mod_final, final_gain):
    bsz, s, d = x.shape
    depth = w_in.shape[0]
    sb_w = sb_out_gain.shape[1]
    dsa_w = dsa_out_gain.shape[1]
    kv_w = kv_gain.shape[1]
    topk = min(TOPK_MAX, s // 4)
    assert s % max(SB_KC, DSA_KC) == 0 and topk % Q_BLOCK == 0
    assert HEAD_DIM * 2 == LANES and w_uk.shape[2] == HEAD_DIM and s // LANES <= 256

    mod = _mod(c, w_mod, b_mod).reshape(depth, bsz, 6, d)
    fmod = _mod(c, w_mod_final[None], b_mod_final[None]).reshape(bsz, 2, d)

    n_a = 3 * sb_w + dsa_w + kv_w
    n_idx = N_IDX_HEADS * IDX_DIM + IDX_DIM + N_IDX_HEADS
    idx_pad = N_IDX_HEADS * IDX_DIM + LANES - n_idx
    bkv = jnp.concatenate([jnp.zeros((3 * HEAD_DIM,), F32), jnp.ones((HEAD_DIM,), F32)])[None]

    for l in range(depth):
        wa = w_in[l][:, :n_a].astype(BF16)
        wi = jnp.pad(w_in[l][:, n_a:], ((0, 0), (0, idx_pad)))
        wih = wi.astype(BF16)
        wil = (wi - wih.astype(F32)).astype(BF16)
        wkv = jnp.concatenate([w_uk[l], w_uk[l], w_uv[l], jnp.zeros_like(w_uv[l])],
                              axis=1).astype(BF16)
        (sbq, sbk, sbv, dq, kda, qi2, ki1, wt, vd1t) = _in_proj(
            x, mod[l], norm1_gain[l][None], wa, wih, wil, kv_gain[l][None], wkv, bkv,
            sb_w=sb_w, dsa_w=dsa_w, kv_w=kv_w)
        osb = _sb_attention(sbq, sbk, sbv)
        odsa = _dsa_attention(dq, kda, vd1t, qi2, ki1, wt, topk=topk)
        x = _out_ffn(x, osb, odsa, mod[l], sb_out_gain[l][None], dsa_out_gain[l][None],
                     w_o[l].astype(BF16), norm2_gain[l][None], w_gate[l].astype(BF16),
                     w_up[l].astype(BF16), w_down[l].astype(BF16), final_gain[None], fmod,
                     final=(l == depth - 1))
    return x
```

```python
import functools

import jax
import jax.numpy as jnp
from jax import lax
from jax.experimental import pallas as pl
from jax.experimental.pallas import tpu as pltpu

F32 = jnp.float32
BF16 = jnp.bfloat16
I32 = jnp.int32

HEAD_DIM = 64
LANES = 128
Q_BLOCK = 128
N_IDX_HEADS = 4
IDX_DIM = 64
TOPK_MAX = 256
RMS_EPS = 1e-6
INT_MIN = -2 ** 31
NEG_BIG = -1e30
VMEM_LIMIT = 52 * 1024 * 1024
SB_KC = 512
SB_ROW_TILE = 64
DSA_KC = 512


def _nt(a, b):
    return lax.dot_general(a, b, (((1,), (1,)), ((), ())), preferred_element_type=F32)


def _dot(a, b):
    return jnp.dot(a, b, preferred_element_type=F32)


def _split(x):
    hi = x.astype(BF16)
    lo = (x - hi.astype(F32)).astype(BF16)
    return hi, lo


def _rms(x, gain):
    ms = jnp.mean(x * x, axis=-1, keepdims=True)
    return x * lax.rsqrt(ms + RMS_EPS) * gain


def _mod_kernel(c_ref, w_ref, b_ref, o_ref):
    c = c_ref[...]
    ca = c / (1.0 + jnp.exp(-c))
    ch, cl = _split(ca)
    wh, wl = _split(w_ref[0])
    o_ref[0] = _dot(ch, wh) + _dot(ch, wl) + _dot(cl, wh) + b_ref[0]


def _mod(c, w, b, tn=1024):
    nl, d, n = w.shape
    bsz = c.shape[0]
    return pl.pallas_call(
        _mod_kernel,
        grid=(nl, n // tn),
        in_specs=[pl.BlockSpec((bsz, d), lambda l, j: (0, 0)),
                  pl.BlockSpec((1, d, tn), lambda l, j: (l, 0, j)),
                  pl.BlockSpec((1, 1, tn), lambda l, j: (l, 0, j))],
        out_specs=pl.BlockSpec((1, bsz, tn), lambda l, j: (l, 0, j)),
        out_shape=jax.ShapeDtypeStruct((nl, bsz, n), F32),
        name="adaln_mod",
    )(c, w, b.reshape(nl, 1, n))


def _in_kernel(x_ref, mod_ref, g_ref, wa_ref, wih_ref, wil_ref, kvg_ref, wkv_ref, bkv_ref,
               sbq_ref, sbk_ref, sbv_ref, dq_ref, kda_ref, qi2_ref, ki1_ref, wt_ref, vd1t_ref,
               *, sb_w, dsa_w, kv_w):
    tm = x_ref.shape[1]
    x = x_ref[0]
    mod = mod_ref[0]
    h = _rms(x, g_ref[...]) * (1.0 + mod[1:2]) + mod[0:1]
    hh, hl = _split(h)
    main = _dot(hh, wa_ref[...])
    scale = HEAD_DIM ** -0.5
    sbq_ref[0] = (main[:, 0:sb_w] * scale).astype(BF16)
    sbk_ref[0] = main[:, sb_w:2 * sb_w].astype(BF16)
    sbv_ref[0] = main[:, 2 * sb_w:3 * sb_w].astype(BF16)
    o = 3 * sb_w
    dq_ref[0] = (main[:, o:o + dsa_w] * scale).astype(BF16)
    o += dsa_w
    kv = _rms(main[:, o:o + kv_w], kvg_ref[...])
    kvp = _dot(kv.astype(BF16), wkv_ref[...]) + bkv_ref[...]
    lane = lax.broadcasted_iota(I32, (tm, LANES), 1)
    pos = pl.program_id(1) * tm + lax.broadcasted_iota(I32, (tm, LANES), 0)
    posf = jnp.where(lane == 0, pos >> 7, jnp.where(lane == 1, pos & (LANES - 1), 0)).astype(F32)
    kda_ref[0, :, 0:LANES] = kvp[:, 0:LANES].astype(BF16)
    kda_ref[0, :, LANES:2 * LANES] = posf.astype(BF16)
    vd1t_ref[0] = kvp[:, LANES:2 * LANES].T.astype(BF16)
    wih = wih_ref[...]
    idx = _dot(hh, wih) + _dot(hh, wil_ref[...]) + _dot(hl, wih)
    first = lane < IDX_DIM
    for pr in range(N_IDX_HEADS // 2):
        pair = idx[:, pr * LANES:(pr + 1) * LANES] * (IDX_DIM ** -0.5)
        rolled = pltpu.roll(pair, IDX_DIM, 1)
        for half in range(2):
            dup = jnp.where(first, pair, rolled) if half == 0 else jnp.where(first, rolled, pair)
            qh, ql = _split(dup)
            hd = 2 * pr + half
            qi2_ref[0, :, 2 * hd * LANES:(2 * hd + 1) * LANES] = qh
            qi2_ref[0, :, (2 * hd + 1) * LANES:(2 * hd + 2) * LANES] = jnp.where(
                first, ql, jnp.zeros_like(ql))
    nq = N_IDX_HEADS * IDX_DIM
    kw = idx[:, nq:nq + LANES]
    kh, kl = _split(jnp.where(first, kw, pltpu.roll(kw, IDX_DIM, 1)))
    ki1_ref[0] = jnp.where(first, kh, kl)
    kwt = kw.T
    wt_ref[0] = kwt[IDX_DIM:IDX_DIM + 8, :] * (N_IDX_HEADS ** -0.5)


def _in_proj(x, mod, gain, wa, wih, wil, kvg, wkv, bkv, *, sb_w, dsa_w, kv_w, tm=512):
    bsz, s, d = x.shape
    na = wa.shape[1]
    ni = wih.shape[1]
    nkv = wkv.shape[1]
    full = lambda shape: pl.BlockSpec(shape, lambda b, i: (0,) * len(shape))
    row = lambda w: pl.BlockSpec((1, tm, w), lambda b, i: (b, i, 0))
    outs = [
        (sb_w, BF16), (sb_w, BF16), (sb_w, BF16), (dsa_w, BF16),
        (2 * LANES, BF16), (2 * LANES * N_IDX_HEADS, BF16), (LANES, BF16),
    ]
    out_shape = [jax.ShapeDtypeStruct((bsz, s, w), dt) for w, dt in outs]
    out_specs = [row(w) for w, _ in outs]
    for rows, dt in ((8, F32), (LANES, BF16)):
        out_shape.append(jax.ShapeDtypeStruct((bsz, rows, s), dt))
        out_specs.append(pl.BlockSpec((1, rows, tm), lambda b, i: (b, 0, i)))
    return pl.pallas_call(
        functools.partial(_in_kernel, sb_w=sb_w, dsa_w=dsa_w, kv_w=kv_w),
        grid=(bsz, s // tm),
        in_specs=[row(d),
                  pl.BlockSpec((1, 6, d), lambda b, i: (b, 0, 0)),
                  full((1, d)), full((d, na)), full((d, ni)), full((d, ni)),
                  full((1, kv_w)), full((kv_w, nkv)), full((1, nkv))],
        out_specs=out_specs,
        out_shape=out_shape,
        compiler_params=pltpu.CompilerParams(
            dimension_semantics=("parallel", "parallel"), vmem_limit_bytes=VMEM_LIMIT),
        name="in_proj",
    )(x, mod, gain, wa, wih, wil, kvg, wkv, bkv)


def _sb_kernel(q_ref, k_ref, v_ref, o_ref, car_ref, *, npair):
    i = pl.program_id(1)
    t = Q_BLOCK
    kc = SB_KC
    nsub = kc // LANES
    lane = lax.broadcasted_iota(I32, (t, LANES), 1)
    first = lane < HEAD_DIM
    zb = jnp.zeros((t, LANES), BF16)
    r2 = lax.broadcasted_iota(I32, (LANES, 2 * LANES), 0)
    c2 = lax.broadcasted_iota(I32, (LANES, 2 * LANES), 1)
    uo = jnp.where((c2 >= LANES) | (r2 > c2), 1.0, 0.0).astype(BF16)
    ct = i // nsub
    rt = SB_ROW_TILE

    def stage_scores(pr, c, ns):
        cols = slice(pr * LANES, (pr + 1) * LANES)
        start = pl.multiple_of(c * kc, kc)
        q2 = q_ref[0, :, cols]
        qs = jnp.concatenate([jnp.where(first, q2, zb), jnp.where(first, zb, q2)], axis=0)
        z = _nt(qs, k_ref[0, pl.ds(start, ns * LANES), cols])
        return jnp.concatenate([z[:, j * LANES:(j + 1) * LANES] for j in range(ns)], axis=0)

    def causal_tile(r0):
        rr = (r0 + lax.broadcasted_iota(I32, (rt, LANES), 0)) & (t - 1)
        return lax.broadcasted_iota(I32, (rt, LANES), 1) < rr

    def stage_cumsum(u, top, ns):
        ds, sps = [], []
        for r0 in range(0, ns * 2 * t, rt):
            ut = u[r0:r0 + rt]
            sp = jnp.maximum(ut, 0.0) + jnp.log(1.0 + jnp.exp(-jnp.abs(ut)))
            ds.append(ut - sp)
            if top and r0 >= (ns - 1) * 2 * t:
                sp = jnp.where(causal_tile(r0), sp, 0.0)
            sps.append(sp.astype(BF16))
        spb = jnp.concatenate(sps, axis=0)
        half = ns * t
        cs = jnp.concatenate([_dot(spb[:half], uo), _dot(spb[half:], uo)], axis=0)
        return ds, cs

    def stage_output(pr, c, ds, cs, top, ns):
        cols = slice(pr * LANES, (pr + 1) * LANES)
        start = pl.multiple_of(c * kc, kc)
        v2 = v_ref[0, pl.ds(start, ns * LANES), cols]
        firstk = lax.broadcasted_iota(I32, (ns * LANES, LANES), 1) < HEAD_DIM
        zk = jnp.zeros((ns * LANES, LANES), BF16)
        ntile = 2 * t // rt
        runs = [jnp.zeros((rt, LANES), F32) if top else car_ref[pr, k * rt:(k + 1) * rt]
                for k in range(ntile)]
        pieces = [[None] * ntile for _ in range(ns)]
        for j in reversed(range(ns)):
            for k in range(ntile):
                r0 = j * 2 * t + k * rt
                a = jnp.exp(ds[r0 // rt] - (cs[r0:r0 + rt, :LANES] + runs[k]))
                if top and j == ns - 1:
                    a = jnp.where(causal_tile(r0), a, 0.0)
                pieces[j][k] = a.astype(BF16)
                runs[k] = runs[k] + cs[r0:r0 + rt, LANES:]
        for k in range(ntile):
            car_ref[pr, k * rt:(k + 1) * rt] = runs[k]
        pieces = [jnp.concatenate(p, axis=0) for p in pieces]
        a_e = jnp.concatenate([p[:t] for p in pieces], axis=1)
        a_o = jnp.concatenate([p[t:] for p in pieces], axis=1)
        pv = _dot(a_e, jnp.where(firstk, v2, zk)) + _dot(a_o, jnp.where(firstk, zk, v2))
        if top:
            o_ref[0, :, cols] = pv
        else:
            o_ref[0, :, cols] += pv

    def all_pairs(c, top, ns):
        us, mids = {}, {}
        for step in range(npair + 2):
            if step < npair:
                us[step] = stage_scores(step, c, ns)
            if 0 <= step - 1 < npair:
                mids[step - 1] = stage_cumsum(us.pop(step - 1), top, ns)
            if 0 <= step - 2 < npair:
                stage_output(step - 2, c, *mids.pop(step - 2), top, ns)

    for r in range(nsub):
        @pl.when(i % nsub == r)
        def _(r=r):
            all_pairs(ct, True, r + 1)

    def body(j, carry):
        all_pairs(ct - 1 - j, False, nsub)
        return carry

    lax.fori_loop(0, ct, body, 0)


def _sb_attention(q, k, v):
    bsz, s, w = q.shape
    npair = w // LANES
    nq = s // Q_BLOCK
    return pl.pallas_call(
        functools.partial(_sb_kernel, npair=npair),
        grid=(bsz, nq),
        in_specs=[pl.BlockSpec((1, Q_BLOCK, w), lambda b, i: (b, i, 0)),
                  pl.BlockSpec((1, s, w), lambda b, i: (b, 0, 0)),
                  pl.BlockSpec((1, s, w), lambda b, i: (b, 0, 0))],
        out_specs=pl.BlockSpec((1, Q_BLOCK, w), lambda b, i: (b, i, 0)),
        out_shape=jax.ShapeDtypeStruct((bsz, s, w), F32),
        scratch_shapes=[pltpu.VMEM((npair, 2 * Q_BLOCK, LANES), F32)],
        compiler_params=pltpu.CompilerParams(
            dimension_semantics=("parallel", "arbitrary"), vmem_limit_bytes=VMEM_LIMIT),
        name="sb_attention",
    )(q, k, v)


def _key_to_float(k):
    return lax.bitcast_convert_type(k ^ ((k >> 31) & 0x7FFFFFFF), F32)


def _dsa_kernel(dq_ref, kda_ref, vd1t_ref, qi2_ref, ki1_ref, wt_ref, o_ref,
                key_ref, qa_ref, acc_ref, m_ref, thr_ref, sc_ref, *, topk, n_heads):
    i = pl.program_id(1)
    t = Q_BLOCK
    kc = DSA_KC
    nsub = kc // t
    n_full = (i + 1) // nsub
    rem = (i + 1) % nsub
    n_kc = n_full + jnp.minimum(rem, 1)
    kf = float(topk)
    lane = lax.broadcasted_iota(I32, (t, LANES), 1)
    row = lax.broadcasted_iota(I32, (t, LANES), 0)
    first = lane < HEAD_DIM

    zb = jnp.zeros((t, LANES), BF16)
    for hd in range(n_heads):
        q2 = dq_ref[0, :, (hd // 2) * LANES:(hd // 2 + 1) * LANES]
        slope = 2.0 ** (-8.0 * (hd + 1) / n_heads)
        aug = jnp.where(lane == 0, LANES * slope, jnp.where(lane == 1, slope, 0.0)).astype(BF16)
        qa_ref[hd * t:(hd + 1) * t, 0:LANES] = (
            jnp.where(first, q2, zb) if hd % 2 == 0 else jnp.where(first, zb, q2))
        qa_ref[hd * t:(hd + 1) * t, LANES:2 * LANES] = aug

    wt = wt_ref[0]
    rowlane = (lax.broadcasted_iota(I32, (kc, LANES), 0)
               - lax.broadcasted_iota(I32, (kc, LANES), 1))

    def score_rows(start, rows):
        sc_ref[pl.ds(start, rows), :] = _nt(kda_ref[0, pl.ds(start, rows), :], qa_ref[...])
        k1 = ki1_ref[0, pl.ds(start, rows), :]
        lhs = jnp.concatenate([k1, k1], axis=1)
        score = None
        for pr in range(N_IDX_HEADS // 2):
            w2 = jnp.concatenate(
                [qi2_ref[0, :, 2 * hd * LANES:(2 * hd + 2) * LANES] for hd in (2 * pr, 2 * pr + 1)],
                axis=0)
            lg = _nt(lhs, w2)
            for half in range(2):
                hd = 2 * pr + half
                term = jnp.maximum(lg[:, half * LANES:(half + 1) * LANES], 0.0) * wt[hd:hd + 1, :]
                score = term if score is None else score + term
        key_ref[pl.ds(start, rows), :] = jnp.where(rowlane[:rows] > i * t - start, -jnp.inf, score)

    def score_chunk(c, carry):
        score_rows(pl.multiple_of(c * kc, kc), kc)
        return carry

    lax.fori_loop(0, n_full, score_chunk, 0)
    rem_start = pl.multiple_of(n_full * kc, kc)
    for r in range(1, nsub):
        @pl.when(rem == r)
        def _(r=r):
            score_rows(rem_start, r * t)
            key_ref[pl.ds(rem_start + r * t, kc - r * t), :] = jnp.full(
                (kc - r * t, LANES), -jnp.inf, F32)

    def count(pred):
        def body(c, acc):
            start = pl.multiple_of(c * kc, kc)
            m = jnp.where(pred(key_ref[pl.ds(start, kc), :]), 1.0, 0.0)
            m = m.reshape(8, kc // 8, LANES).sum(axis=0)
            return acc + m.reshape(kc // 64, 8, LANES).sum(axis=0)
        part = lax.fori_loop(0, n_kc, body, jnp.zeros((8, LANES), F32))
        return jnp.sum(part, axis=0, keepdims=True)

    searching = (i + 1) * t > topk

    def search(n128):
        pieces = [(lo_r, min(lo_r + kc, n128 * t)) for lo_r in range(0, n128 * t, kc)]

        def bit_step(bi, lo):
            cand = lo + lax.shift_left(jnp.int32(1), 31 - bi)
            cand_f = _key_to_float(cand)
            acc = None
            for lo_r, hi_r in pieces:
                m = jnp.where(key_ref[lo_r:hi_r, :] >= cand_f, 1.0, 0.0)
                m = m.reshape(8, (hi_r - lo_r) // 8, LANES).sum(axis=0)
                m = m.reshape((hi_r - lo_r) // 64, 8, LANES).sum(axis=0)
                acc = m if acc is None else acc + m
            cnt = jnp.sum(acc, axis=0, keepdims=True)
            return jnp.where(cnt >= kf, cand, lo)

        return _key_to_float(lax.fori_loop(0, 32, bit_step, jnp.full((1, LANES), INT_MIN, I32)))

    thr_ref[...] = jnp.full(thr_ref.shape, jnp.finfo(F32).min, F32)
    for n128 in range(topk // t + 1, key_ref.shape[0] // t + 1):
        @pl.when(i + 1 == n128)
        def _(n128=n128):
            thr_ref[...] = jnp.broadcast_to(search(n128), thr_ref.shape)

    thr = thr_ref[0:1, :]
    cnt_ge = count(lambda blk: blk >= thr)

    @pl.when((jnp.max(cnt_ge) > kf) & searching)
    def _():
        need = kf - count(lambda blk: blk > thr)
        sl = jnp.where(lane < row, 1.0, 0.0).astype(BF16)
        nsub = kc // t

        def body(c, before):
            start = pl.multiple_of(c * kc, kc)
            subs = [key_ref[pl.ds(start + j * t, t), :] for j in range(nsub)]
            eqfs = [jnp.where(sub == thr, 1.0, 0.0) for sub in subs]
            ranks = [_dot(sl, eqf.astype(BF16)) for eqf in eqfs]
            for j in range(nsub):
                demote = (subs[j] == thr) & (ranks[j] + before >= need)
                key_ref[pl.ds(start + j * t, t), :] = jnp.where(demote, -jnp.inf, subs[j])
                before = before + jnp.sum(eqfs[j], axis=0, keepdims=True)
            return before

        lax.fori_loop(0, n_kc, body, jnp.zeros((1, LANES), F32))

    m_ref[...] = jnp.full(m_ref.shape, NEG_BIG, F32)
    acc_ref[...] = jnp.zeros(acc_ref.shape, F32)

    def attn_rows(start, rows):
        vt = vd1t_ref[0, :, pl.ds(start, rows)]
        bias = jnp.where(key_ref[pl.ds(start, rows), :] >= thr, 0.0, -jnp.inf)
        sc = sc_ref[pl.ds(start, rows), :] + jnp.concatenate([bias] * n_heads, axis=1)
        m_old = m_ref[...]
        m_new = jnp.maximum(m_old, jnp.max(sc, axis=0, keepdims=True))
        acc_ref[...] = (acc_ref[...] * jnp.exp(m_old - m_new)
                        + _dot(vt, jnp.exp(sc - m_new).astype(BF16)))
        m_ref[...] = m_new

    def attn_chunk(c, carry):
        attn_rows(pl.multiple_of(c * kc, kc), kc)
        return carry

    lax.fori_loop(0, n_full, attn_chunk, 0)
    for r in range(1, nsub):
        @pl.when(rem == r)
        def _(r=r):
            attn_rows(rem_start, r * t)

    for pr in range(n_heads // 2):
        a_e = acc_ref[:, (2 * pr) * t:(2 * pr + 1) * t]
        a_o = acc_ref[:, (2 * pr + 1) * t:(2 * pr + 2) * t]
        r = jnp.concatenate([a_e[:HEAD_DIM] / a_e[HEAD_DIM:], a_o[:HEAD_DIM] / a_o[HEAD_DIM:]], axis=0)
        o_ref[0, :, pr * LANES:(pr + 1) * LANES] = r.T


def _dsa_attention(dq, kda, vd1t, qi2, ki1, wt, *, topk):
    bsz, s, w = dq.shape
    nq = s // Q_BLOCK
    n_heads = w // HEAD_DIM
    blk = lambda wd: pl.BlockSpec((1, Q_BLOCK, wd), lambda b, i: (b, i, 0))
    full = lambda wd: pl.BlockSpec((1, s, wd), lambda b, i: (b, 0, 0))
    return pl.pallas_call(
        functools.partial(_dsa_kernel, topk=topk, n_heads=n_heads),
        grid=(bsz, nq),
        in_specs=[blk(w), full(2 * LANES),
                  pl.BlockSpec((1, LANES, s), lambda b, i: (b, 0, 0)),
                  blk(qi2.shape[2]), full(LANES),
                  pl.BlockSpec((1, 8, Q_BLOCK), lambda b, i: (b, 0, i))],
        out_specs=blk(w),
        out_shape=jax.ShapeDtypeStruct((bsz, s, w), F32),
        scratch_shapes=[pltpu.VMEM((s, LANES), F32),
                        pltpu.VMEM((n_heads * Q_BLOCK, 2 * LANES), BF16),
                        pltpu.VMEM((LANES, n_heads * Q_BLOCK), F32),
                        pltpu.VMEM((1, n_heads * Q_BLOCK), F32),
                        pltpu.VMEM((8, LANES), F32),
                        pltpu.VMEM((s, n_heads * Q_BLOCK), F32)],
        compiler_params=pltpu.CompilerParams(
            dimension_semantics=("parallel", "arbitrary"), vmem_limit_bytes=VMEM_LIMIT),
        name="dsa_attention",
    )(dq, kda, vd1t, qi2, ki1, wt)


def _ffn_kernel(x_ref, osb_ref, odsa_ref, mod_ref, gsb_ref, gdsa_ref, wo_ref, g_ref,
                wg_ref, wu_ref, wd_ref, fg_ref, fmod_ref, o_ref,
                x1_ref, h_ref, acc_ref, *, final):
    j = pl.program_id(2)

    @pl.when(j == 0)
    def _():
        mod = mod_ref[0]
        sb_w = osb_ref.shape[2]
        a = _rms(osb_ref[0], gsb_ref[...]).astype(BF16)
        b = _rms(odsa_ref[0], gdsa_ref[...]).astype(BF16)
        y = _dot(a, wo_ref[0:sb_w, :]) + _dot(b, wo_ref[sb_w:, :])
        x1 = x_ref[0] + mod[2:3] * y
        x1_ref[...] = x1
        h = _rms(x1, g_ref[...]) * (1.0 + mod[4:5]) + mod[3:4]
        h_ref[...] = h.astype(BF16)

    h = h_ref[...]
    g = _dot(h, wg_ref[...])
    u = _dot(h, wu_ref[...])
    act = (g / (1.0 + jnp.exp(-g))) * u
    part = _dot(act.astype(BF16), wd_ref[...])

    @pl.when(j == 0)
    def _():
        acc_ref[...] = part

    @pl.when(j > 0)
    def _():
        acc_ref[...] += part

    @pl.when(j == pl.num_programs(2) - 1)
    def _():
        y = x1_ref[...] + mod_ref[0][5:6] * acc_ref[...]
        if final:
            fmod = fmod_ref[0]
            y = _rms(y, fg_ref[...]) * (1.0 + fmod[1:2]) + fmod[0:1]
        o_ref[0] = y


def _out_ffn(x, osb, odsa, mod, gsb, gdsa, wo, gain, wg, wu, wd, fgain, fmod, *, final,
             tm=512, tf=1408):
    bsz, s, d = x.shape
    dff = wg.shape[1]
    sb_w, dsa_w = osb.shape[2], odsa.shape[2]
    tm = min(tm, s)
    row = lambda w: pl.BlockSpec((1, tm, w), lambda b, i, j: (b, i, 0))
    full = lambda shape: pl.BlockSpec(shape, lambda b, i, j: (0,) * len(shape))
    return pl.pallas_call(
        functools.partial(_ffn_kernel, final=final),
        grid=(bsz, s // tm, dff // tf),
        in_specs=[row(d), row(sb_w), row(dsa_w),
                  pl.BlockSpec((1, 6, d), lambda b, i, j: (b, 0, 0)),
                  full((1, sb_w)), full((1, dsa_w)), full((sb_w + dsa_w, d)), full((1, d)),
                  pl.BlockSpec((d, tf), lambda b, i, j: (0, j)),
                  pl.BlockSpec((d, tf), lambda b, i, j: (0, j)),
                  pl.BlockSpec((tf, d), lambda b, i, j: (j, 0)),
                  full((1, d)),
                  pl.BlockSpec((1, 2, d), lambda b, i, j: (b, 0, 0))],
        out_specs=row(d),
        out_shape=jax.ShapeDtypeStruct((bsz, s, d), F32),
        scratch_shapes=[pltpu.VMEM((tm, d), F32), pltpu.VMEM((tm, d), BF16),
                        pltpu.VMEM((tm, d), F32)],
        compiler_params=pltpu.CompilerParams(
            dimension_semantics=("parallel", "parallel", "arbitrary"),
            vmem_limit_bytes=VMEM_LIMIT),
        name="out_ffn",
    )(x, osb, odsa, mod, gsb, gdsa, wo, gain, wg, wu, wd, fgain, fmod)


def kernel(x, c, w_mod, b_mod, norm1_gain, norm2_gain, w_in, kv_gain, w_uk, w_uv, sb_out_gain,
           dsa_out_gain, w_o, w_gate, w_up, w_down, w_mod_final, b_mod_final, final_gain):
    bsz, s, d = x.shape
    depth = w_in.shape[0]
    sb_w = sb_out_gain.shape[1]
    dsa_w = dsa_out_gain.shape[1]
    kv_w = kv_gain.shape[1]
    topk = min(TOPK_MAX, s // 4)
    assert s % max(SB_KC, DSA_KC) == 0 and topk % Q_BLOCK == 0
    assert HEAD_DIM * 2 == LANES and w_uk.shape[2] == HEAD_DIM and s // LANES <= 256

    mod = _mod(c, w_mod, b_mod).reshape(depth, bsz, 6, d)
    fmod = _mod(c, w_mod_final[None], b_mod_final[None]).reshape(bsz, 2, d)

    n_a = 3 * sb_w + dsa_w + kv_w
    n_idx = N_IDX_HEADS * IDX_DIM + IDX_DIM + N_IDX_HEADS
    idx_pad = N_IDX_HEADS * IDX_DIM + LANES - n_idx
    bkv = jnp.concatenate([jnp.zeros((3 * HEAD_DIM,), F32), jnp.ones((HEAD_DIM,), F32)])[None]

    for l in range(depth):
        wa = w_in[l][:, :n_a].astype(BF16)
        wi = jnp.pad(w_in[l][:, n_a:], ((0, 0), (0, idx_pad)))
        wih = wi.astype(BF16)
        wil = (wi - wih.astype(F32)).astype(BF16)
        wkv = jnp.concatenate([w_uk[l], w_uk[l], w_uv[l], jnp.zeros_like(w_uv[l])],
                              axis=1).astype(BF16)
        (sbq, sbk, sbv, dq, kda, qi2, ki1, wt, vd1t) = _in_proj(
            x, mod[l], norm1_gain[l][None], wa, wih, wil, kv_gain[l][None], wkv, bkv,
            sb_w=sb_w, dsa_w=dsa_w, kv_w=kv_w)
        osb = _sb_attention(sbq, sbk, sbv)
        odsa = _dsa_attention(dq, kda, vd1t, qi2, ki1, wt, topk=topk)
        x = _out_ffn(x, osb, odsa, mod[l], sb_out_gain[l][None], dsa_out_gain[l][None],
                     w_o[l].astype(BF16), norm2_gain[l][None], w_gate[l].astype(BF16),
                     w_up[l].astype(BF16), w_down[l].astype(BF16), final_gain[None], fmod,
                     final=(l == depth - 1))
    return x
```

```python
import functools

import jax
import jax.numpy as jnp
from jax import lax
from jax.experimental import pallas as pl
from jax.experimental.pallas import tpu as pltpu

F32 = jnp.float32
BF16 = jnp.bfloat16
I32 = jnp.int32

HEAD_DIM = 64
LANES = 128
Q_BLOCK = 128
N_IDX_HEADS = 4
IDX_DIM = 64
TOPK_MAX = 256
RMS_EPS = 1e-6
INT_MIN = -2 ** 31
NEG_BIG = -1e30
VMEM_LIMIT = 52 * 1024 * 1024
SB_KC = 512
SB_ROW_TILE = 64
SB_UNDERFLOW = 110.0
DSA_KC = 512


def _nt(a, b):
    return lax.dot_general(a, b, (((1,), (1,)), ((), ())), preferred_element_type=F32)


def _dot(a, b):
    return jnp.dot(a, b, preferred_element_type=F32)


def _split(x):
    hi = x.astype(BF16)
    lo = (x - hi.astype(F32)).astype(BF16)
    return hi, lo


def _rms(x, gain):
    ms = jnp.mean(x * x, axis=-1, keepdims=True)
    return x * lax.rsqrt(ms + RMS_EPS) * gain


def _mod_kernel(c_ref, w_ref, b_ref, o_ref):
    c = c_ref[...]
    ca = c / (1.0 + jnp.exp(-c))
    ch, cl = _split(ca)
    wh, wl = _split(w_ref[0])
    o_ref[0] = _dot(ch, wh) + _dot(ch, wl) + _dot(cl, wh) + b_ref[0]


def _mod(c, w, b, tn=1024):
    nl, d, n = w.shape
    bsz = c.shape[0]
    return pl.pallas_call(
        _mod_kernel,
        grid=(nl, n // tn),
        in_specs=[pl.BlockSpec((bsz, d), lambda l, j: (0, 0)),
                  pl.BlockSpec((1, d, tn), lambda l, j: (l, 0, j)),
                  pl.BlockSpec((1, 1, tn), lambda l, j: (l, 0, j))],
        out_specs=pl.BlockSpec((1, bsz, tn), lambda l, j: (l, 0, j)),
        out_shape=jax.ShapeDtypeStruct((nl, bsz, n), F32),
        name="adaln_mod",
    )(c, w, b.reshape(nl, 1, n))


def _in_kernel(x_ref, mod_ref, g_ref, wa_ref, wih_ref, wil_ref, kvg_ref, wkv_ref, bkv_ref,
               sbq_ref, sbk_ref, sbv_ref, dq_ref, kda_ref, qi2_ref, ki1_ref, wt_ref, vd1t_ref,
               *, sb_w, dsa_w, kv_w):
    tm = x_ref.shape[1]
    x = x_ref[0]
    mod = mod_ref[0]
    h = _rms(x, g_ref[...]) * (1.0 + mod[1:2]) + mod[0:1]
    hh, hl = _split(h)
    main = _dot(hh, wa_ref[...])
    scale = HEAD_DIM ** -0.5
    sbq_ref[0] = (main[:, 0:sb_w] * scale).astype(BF16)
    sbk_ref[0] = main[:, sb_w:2 * sb_w].astype(BF16)
    sbv_ref[0] = main[:, 2 * sb_w:3 * sb_w].astype(BF16)
    o = 3 * sb_w
    dq_ref[0] = (main[:, o:o + dsa_w] * scale).astype(BF16)
    o += dsa_w
    kv = _rms(main[:, o:o + kv_w], kvg_ref[...])
    kvp = _dot(kv.astype(BF16), wkv_ref[...]) + bkv_ref[...]
    lane = lax.broadcasted_iota(I32, (tm, LANES), 1)
    pos = pl.program_id(1) * tm + lax.broadcasted_iota(I32, (tm, LANES), 0)
    posf = jnp.where(lane == 0, pos >> 7, jnp.where(lane == 1, pos & (LANES - 1), 0)).astype(F32)
    kda_ref[0, :, 0:LANES] = kvp[:, 0:LANES].astype(BF16)
    kda_ref[0, :, LANES:2 * LANES] = posf.astype(BF16)
    vd1t_ref[0] = kvp[:, LANES:2 * LANES].T.astype(BF16)
    wih = wih_ref[...]
    idx = _dot(hh, wih) + _dot(hh, wil_ref[...]) + _dot(hl, wih)
    first = lane < IDX_DIM
    for pr in range(N_IDX_HEADS // 2):
        pair = idx[:, pr * LANES:(pr + 1) * LANES] * (IDX_DIM ** -0.5)
        rolled = pltpu.roll(pair, IDX_DIM, 1)
        for half in range(2):
            dup = jnp.where(first, pair, rolled) if half == 0 else jnp.where(first, rolled, pair)
            qh, ql = _split(dup)
            hd = 2 * pr + half
            qi2_ref[0, :, 2 * hd * LANES:(2 * hd + 1) * LANES] = qh
            qi2_ref[0, :, (2 * hd + 1) * LANES:(2 * hd + 2) * LANES] = jnp.where(
                first, ql, jnp.zeros_like(ql))
    nq = N_IDX_HEADS * IDX_DIM
    kw = idx[:, nq:nq + LANES]
    kh, kl = _split(jnp.where(first, kw, pltpu.roll(kw, IDX_DIM, 1)))
    ki1_ref[0] = jnp.where(first, kh, kl)
    kwt = kw.T
    wt_ref[0] = kwt[IDX_DIM:IDX_DIM + 8, :] * (N_IDX_HEADS ** -0.5)


def _in_proj(x, mod, gain, wa, wih, wil, kvg, wkv, bkv, *, sb_w, dsa_w, kv_w, tm=512):
    bsz, s, d = x.shape
    na = wa.shape[1]
    ni = wih.shape[1]
    nkv = wkv.shape[1]
    full = lambda shape: pl.BlockSpec(shape, lambda b, i: (0,) * len(shape))
    row = lambda w: pl.BlockSpec((1, tm, w), lambda b, i: (b, i, 0))
    outs = [
        (sb_w, BF16), (sb_w, BF16), (sb_w, BF16), (dsa_w, BF16),
        (2 * LANES, BF16), (2 * LANES * N_IDX_HEADS, BF16), (LANES, BF16),
    ]
    out_shape = [jax.ShapeDtypeStruct((bsz, s, w), dt) for w, dt in outs]
    out_specs = [row(w) for w, _ in outs]
    for rows, dt in ((8, F32), (LANES, BF16)):
        out_shape.append(jax.ShapeDtypeStruct((bsz, rows, s), dt))
        out_specs.append(pl.BlockSpec((1, rows, tm), lambda b, i: (b, 0, i)))
    return pl.pallas_call(
        functools.partial(_in_kernel, sb_w=sb_w, dsa_w=dsa_w, kv_w=kv_w),
        grid=(bsz, s // tm),
        in_specs=[row(d),
                  pl.BlockSpec((1, 6, d), lambda b, i: (b, 0, 0)),
                  full((1, d)), full((d, na)), full((d, ni)), full((d, ni)),
                  full((1, kv_w)), full((kv_w, nkv)), full((1, nkv))],
        out_specs=out_specs,
        out_shape=out_shape,
        compiler_params=pltpu.CompilerParams(
            dimension_semantics=("parallel", "parallel"), vmem_limit_bytes=VMEM_LIMIT),
        name="in_proj",
    )(x, mod, gain, wa, wih, wil, kvg, wkv, bkv)


def _sb_kernel(q_ref, k_ref, v_ref, o_ref, car_ref, *, npair):
    i = pl.program_id(1)
    t = Q_BLOCK
    kc = SB_KC
    nsub = kc // LANES
    lane = lax.broadcasted_iota(I32, (t, LANES), 1)
    first = lane < HEAD_DIM
    zb = jnp.zeros((t, LANES), BF16)
    r2 = lax.broadcasted_iota(I32, (LANES, 2 * LANES), 0)
    c2 = lax.broadcasted_iota(I32, (LANES, 2 * LANES), 1)
    uo = jnp.where((c2 >= LANES) | (r2 > c2), 1.0, 0.0).astype(BF16)
    ct = i // nsub
    rt = SB_ROW_TILE

    def stage_scores(pr, c, ns):
        cols = slice(pr * LANES, (pr + 1) * LANES)
        start = pl.multiple_of(c * kc, kc)
        q2 = q_ref[0, :, cols]
        qs = jnp.concatenate([jnp.where(first, q2, zb), jnp.where(first, zb, q2)], axis=0)
        z = _nt(qs, k_ref[0, pl.ds(start, ns * LANES), cols])
        return jnp.concatenate([z[:, j * LANES:(j + 1) * LANES] for j in range(ns)], axis=0)

    def causal_tile(r0):
        rr = (r0 + lax.broadcasted_iota(I32, (rt, LANES), 0)) & (t - 1)
        return lax.broadcasted_iota(I32, (rt, LANES), 1) < rr

    def stage_cumsum(u, top, ns):
        ds, sps = [], []
        for r0 in range(0, ns * 2 * t, rt):
            ut = u[r0:r0 + rt]
            sp = jnp.maximum(ut, 0.0) + jnp.log(1.0 + jnp.exp(-jnp.abs(ut)))
            ds.append(ut - sp)
            if top and r0 >= (ns - 1) * 2 * t:
                sp = jnp.where(causal_tile(r0), sp, 0.0)
            sps.append(sp.astype(BF16))
        spb = jnp.concatenate(sps, axis=0)
        half = ns * t
        cs = jnp.concatenate([_dot(spb[:half], uo), _dot(spb[half:], uo)], axis=0)
        return ds, cs

    def stage_output(pr, c, ds, cs, top, ns):
        cols = slice(pr * LANES, (pr + 1) * LANES)
        start = pl.multiple_of(c * kc, kc)
        v2 = v_ref[0, pl.ds(start, ns * LANES), cols]
        firstk = lax.broadcasted_iota(I32, (ns * LANES, LANES), 1) < HEAD_DIM
        zk = jnp.zeros((ns * LANES, LANES), BF16)
        ntile = 2 * t // rt
        runs = [jnp.zeros((rt, LANES), F32) if top else car_ref[pr, k * rt:(k + 1) * rt]
                for k in range(ntile)]
        pieces = [[None] * ntile for _ in range(ns)]
        for j in reversed(range(ns)):
            for k in range(ntile):
                r0 = j * 2 * t + k * rt
                a = jnp.exp(ds[r0 // rt] - (cs[r0:r0 + rt, :LANES] + runs[k]))
                if top and j == ns - 1:
                    a = jnp.where(causal_tile(r0), a, 0.0)
                pieces[j][k] = a.astype(BF16)
                runs[k] = runs[k] + cs[r0:r0 + rt, LANES:]
        for k in range(ntile):
            car_ref[pr, k * rt:(k + 1) * rt] = runs[k]
        pieces = [jnp.concatenate(p, axis=0) for p in pieces]
        a_e = jnp.concatenate([p[:t] for p in pieces], axis=1)
        a_o = jnp.concatenate([p[t:] for p in pieces], axis=1)
        pv = _dot(a_e, jnp.where(firstk, v2, zk)) + _dot(a_o, jnp.where(firstk, zk, v2))
        if top:
            o_ref[0, :, cols] = pv
        else:
            o_ref[0, :, cols] += pv
        return functools.reduce(jnp.minimum, runs)

    def all_pairs(c, top, ns):
        us, mids, mins = {}, {}, []
        for step in range(npair + 2):
            if step < npair:
                us[step] = stage_scores(step, c, ns)
            if 0 <= step - 1 < npair:
                mids[step - 1] = stage_cumsum(us.pop(step - 1), top, ns)
            if 0 <= step - 2 < npair:
                mins.append(stage_output(step - 2, c, *mids.pop(step - 2), top, ns))
        return functools.reduce(jnp.minimum, mins)

    for r in range(nsub):
        @pl.when(i % nsub == r)
        def _(r=r):
            all_pairs(ct, True, r + 1)

    def cond(state):
        j, min_carry = state
        return (j < ct) & (min_carry < SB_UNDERFLOW)

    def body(state):
        j, _ = state
        return j + 1, jnp.min(all_pairs(ct - 1 - j, False, nsub))

    lax.while_loop(cond, body, (jnp.int32(0), jnp.min(car_ref[...])))


def _sb_attention(q, k, v):
    bsz, s, w = q.shape
    npair = w // LANES
    nq = s // Q_BLOCK
    return pl.pallas_call(
        functools.partial(_sb_kernel, npair=npair),
        grid=(bsz, nq),
        in_specs=[pl.BlockSpec((1, Q_BLOCK, w), lambda b, i: (b, i, 0)),
                  pl.BlockSpec((1, s, w), lambda b, i: (b, 0, 0)),
                  pl.BlockSpec((1, s, w), lambda b, i: (b, 0, 0))],
        out_specs=pl.BlockSpec((1, Q_BLOCK, w), lambda b, i: (b, i, 0)),
        out_shape=jax.ShapeDtypeStruct((bsz, s, w), F32),
        scratch_shapes=[pltpu.VMEM((npair, 2 * Q_BLOCK, LANES), F32)],
        compiler_params=pltpu.CompilerParams(
            dimension_semantics=("parallel", "arbitrary"), vmem_limit_bytes=VMEM_LIMIT),
        name="sb_attention",
    )(q, k, v)


def _key_to_float(k):
    return lax.bitcast_convert_type(k ^ ((k >> 31) & 0x7FFFFFFF), F32)


def _dsa_kernel(dq_ref, kda_ref, vd1t_ref, qi2_ref, ki1_ref, wt_ref, o_ref,
                key_ref, qa_ref, acc_ref, m_ref, thr_ref, sc_ref, *, topk, n_heads):
    i = pl.program_id(1)
    t = Q_BLOCK
    kc = DSA_KC
    nsub = kc // t
    n_full = (i + 1) // nsub
    rem = (i + 1) % nsub
    n_kc = n_full + jnp.minimum(rem, 1)
    kf = float(topk)
    lane = lax.broadcasted_iota(I32, (t, LANES), 1)
    row = lax.broadcasted_iota(I32, (t, LANES), 0)
    first = lane < HEAD_DIM

    zb = jnp.zeros((t, LANES), BF16)
    for hd in range(n_heads):
        q2 = dq_ref[0, :, (hd // 2) * LANES:(hd // 2 + 1) * LANES]
        slope = 2.0 ** (-8.0 * (hd + 1) / n_heads)
        aug = jnp.where(lane == 0, LANES * slope, jnp.where(lane == 1, slope, 0.0)).astype(BF16)
        qa_ref[hd * t:(hd + 1) * t, 0:LANES] = (
            jnp.where(first, q2, zb) if hd % 2 == 0 else jnp.where(first, zb, q2))
        qa_ref[hd * t:(hd + 1) * t, LANES:2 * LANES] = aug

    wt = wt_ref[0]
    rowlane = (lax.broadcasted_iota(I32, (kc, LANES), 0)
               - lax.broadcasted_iota(I32, (kc, LANES), 1))

    def score_rows(start, rows):
        sc_ref[pl.ds(start, rows), :] = _nt(kda_ref[0, pl.ds(start, rows), :], qa_ref[...])
        k1 = ki1_ref[0, pl.ds(start, rows), :]
        lhs = jnp.concatenate([k1, k1], axis=1)
        score = None
        for pr in range(N_IDX_HEADS // 2):
            w2 = jnp.concatenate(
                [qi2_ref[0, :, 2 * hd * LANES:(2 * hd + 2) * LANES] for hd in (2 * pr, 2 * pr + 1)],
                axis=0)
            lg = _nt(lhs, w2)
            for half in range(2):
                hd = 2 * pr + half
                term = jnp.maximum(lg[:, half * LANES:(half + 1) * LANES], 0.0) * wt[hd:hd + 1, :]
                score = term if score is None else score + term
        key_ref[pl.ds(start, rows), :] = jnp.where(rowlane[:rows] > i * t - start, -jnp.inf, score)

    def score_chunk(c, carry):
        score_rows(pl.multiple_of(c * kc, kc), kc)
        return carry

    lax.fori_loop(0, n_full, score_chunk, 0)
    rem_start = pl.multiple_of(n_full * kc, kc)
    for r in range(1, nsub):
        @pl.when(rem == r)
        def _(r=r):
            score_rows(rem_start, r * t)
            key_ref[pl.ds(rem_start + r * t, kc - r * t), :] = jnp.full(
                (kc - r * t, LANES), -jnp.inf, F32)

    def count(pred):
        def body(c, acc):
            start = pl.multiple_of(c * kc, kc)
            m = jnp.where(pred(key_ref[pl.ds(start, kc), :]), 1.0, 0.0)
            m = m.reshape(8, kc // 8, LANES).sum(axis=0)
            return acc + m.reshape(kc // 64, 8, LANES).sum(axis=0)
        part = lax.fori_loop(0, n_kc, body, jnp.zeros((8, LANES), F32))
        return jnp.sum(part, axis=0, keepdims=True)

    searching = (i + 1) * t > topk

    def search(n128):
        pieces = [(lo_r, min(lo_r + kc, n128 * t)) for lo_r in range(0, n128 * t, kc)]

        def bit_step(bi, lo):
            cand = lo + lax.shift_left(jnp.int32(1), 31 - bi)
            cand_f = _key_to_float(cand)
            acc = None
            for lo_r, hi_r in pieces:
                m = jnp.where(key_ref[lo_r:hi_r, :] >= cand_f, 1.0, 0.0)
                m = m.reshape(8, (hi_r - lo_r) // 8, LANES).sum(axis=0)
                m = m.reshape((hi_r - lo_r) // 64, 8, LANES).sum(axis=0)
                acc = m if acc is None else acc + m
            cnt = jnp.sum(acc, axis=0, keepdims=True)
            return jnp.where(cnt >= kf, cand, lo)

        return _key_to_float(lax.fori_loop(0, 32, bit_step, jnp.full((1, LANES), INT_MIN, I32)))

    thr_ref[...] = jnp.full(thr_ref.shape, jnp.finfo(F32).min, F32)
    for n128 in range(topk // t + 1, key_ref.shape[0] // t + 1):
        @pl.when(i + 1 == n128)
        def _(n128=n128):
            thr_ref[...] = jnp.broadcast_to(search(n128), thr_ref.shape)

    thr = thr_ref[0:1, :]
    cnt_ge = count(lambda blk: blk >= thr)

    @pl.when((jnp.max(cnt_ge) > kf) & searching)
    def _():
        need = kf - count(lambda blk: blk > thr)
        sl = jnp.where(lane < row, 1.0, 0.0).astype(BF16)
        nsub = kc // t

        def body(c, before):
            start = pl.multiple_of(c * kc, kc)
            subs = [key_ref[pl.ds(start + j * t, t), :] for j in range(nsub)]
            eqfs = [jnp.where(sub == thr, 1.0, 0.0) for sub in subs]
            ranks = [_dot(sl, eqf.astype(BF16)) for eqf in eqfs]
            for j in range(nsub):
                demote = (subs[j] == thr) & (ranks[j] + before >= need)
                key_ref[pl.ds(start + j * t, t), :] = jnp.where(demote, -jnp.inf, subs[j])
                before = before + jnp.sum(eqfs[j], axis=0, keepdims=True)
            return before

        lax.fori_loop(0, n_kc, body, jnp.zeros((1, LANES), F32))

    m_ref[...] = jnp.full(m_ref.shape, NEG_BIG, F32)
    acc_ref[...] = jnp.zeros(acc_ref.shape, F32)

    def attn_rows(start, rows):
        vt = vd1t_ref[0, :, pl.ds(start, rows)]
        bias = jnp.where(key_ref[pl.ds(start, rows), :] >= thr, 0.0, -jnp.inf)
        sc = sc_ref[pl.ds(start, rows), :] + jnp.concatenate([bias] * n_heads, axis=1)
        m_old = m_ref[...]
        m_new = jnp.maximum(m_old, jnp.max(sc, axis=0, keepdims=True))
        acc_ref[...] = (acc_ref[...] * jnp.exp(m_old - m_new)
                        + _dot(vt, jnp.exp(sc - m_new).astype(BF16)))
        m_ref[...] = m_new

    def attn_chunk(c, carry):
        attn_rows(pl.multiple_of(c * kc, kc), kc)
        return carry

    lax.fori_loop(0, n_full, attn_chunk, 0)
    for r in range(1, nsub):
        @pl.when(rem == r)
        def _(r=r):
            attn_rows(rem_start, r * t)

    for pr in range(n_heads // 2):
        a_e = acc_ref[:, (2 * pr) * t:(2 * pr + 1) * t]
        a_o = acc_ref[:, (2 * pr + 1) * t:(2 * pr + 2) * t]
        r = jnp.concatenate([a_e[:HEAD_DIM] / a_e[HEAD_DIM:], a_o[:HEAD_DIM] / a_o[HEAD_DIM:]], axis=0)
        o_ref[0, :, pr * LANES:(pr + 1) * LANES] = r.T


def _dsa_attention(dq, kda, vd1t, qi2, ki1, wt, *, topk):
    bsz, s, w = dq.shape
    nq = s // Q_BLOCK
    n_heads = w // HEAD_DIM
    blk = lambda wd: pl.BlockSpec((1, Q_BLOCK, wd), lambda b, i: (b, i, 0))
    full = lambda wd: pl.BlockSpec((1, s, wd), lambda b, i: (b, 0, 0))
    return pl.pallas_call(
        functools.partial(_dsa_kernel, topk=topk, n_heads=n_heads),
        grid=(bsz, nq),
        in_specs=[blk(w), full(2 * LANES),
                  pl.BlockSpec((1, LANES, s), lambda b, i: (b, 0, 0)),
                  blk(qi2.shape[2]), full(LANES),
                  pl.BlockSpec((1, 8, Q_BLOCK), lambda b, i: (b, 0, i))],
        out_specs=blk(w),
        out_shape=jax.ShapeDtypeStruct((bsz, s, w), F32),
        scratch_shapes=[pltpu.VMEM((s, LANES), F32),
                        pltpu.VMEM((n_heads * Q_BLOCK, 2 * LANES), BF16),
                        pltpu.VMEM((LANES, n_heads * Q_BLOCK), F32),
                        pltpu.VMEM((1, n_heads * Q_BLOCK), F32),
                        pltpu.VMEM((8, LANES), F32),
                        pltpu.VMEM((s, n_heads * Q_BLOCK), F32)],
        compiler_params=pltpu.CompilerParams(
            dimension_semantics=("parallel", "arbitrary"), vmem_limit_bytes=VMEM_LIMIT),
        name="dsa_attention",
    )(dq, kda, vd1t, qi2, ki1, wt)


def _ffn_kernel(x_ref, osb_ref, odsa_ref, mod_ref, gsb_ref, gdsa_ref, wo_ref, g_ref,
                wg_ref, wu_ref, wd_ref, fg_ref, fmod_ref, o_ref,
                x1_ref, h_ref, acc_ref, *, final):
    j = pl.program_id(2)

    @pl.when(j == 0)
    def _():
        mod = mod_ref[0]
        sb_w = osb_ref.shape[2]
        a = _rms(osb_ref[0], gsb_ref[...]).astype(BF16)
        b = _rms(odsa_ref[0], gdsa_ref[...]).astype(BF16)
        y = _dot(a, wo_ref[0:sb_w, :]) + _dot(b, wo_ref[sb_w:, :])
        x1 = x_ref[0] + mod[2:3] * y
        x1_ref[...] = x1
        h = _rms(x1, g_ref[...]) * (1.0 + mod[4:5]) + mod[3:4]
        h_ref[...] = h.astype(BF16)

    h = h_ref[...]
    g = _dot(h, wg_ref[...])
    u = _dot(h, wu_ref[...])
    act = (g / (1.0 + jnp.exp(-g))) * u
    part = _dot(act.astype(BF16), wd_ref[...])

    @pl.when(j == 0)
    def _():
        acc_ref[...] = part

    @pl.when(j > 0)
    def _():
        acc_ref[...] += part

    @pl.when(j == pl.num_programs(2) - 1)
    def _():
        y = x1_ref[...] + mod_ref[0][5:6] * acc_ref[...]
        if final:
            fmod = fmod_ref[0]
            y = _rms(y, fg_ref[...]) * (1.0 + fmod[1:2]) + fmod[0:1]
        o_ref[0] = y


def _out_ffn(x, osb, odsa, mod, gsb, gdsa, wo, gain, wg, wu, wd, fgain, fmod, *, final,
             tm=512, tf=1408):
    bsz, s, d = x.shape
    dff = wg.shape[1]
    sb_w, dsa_w = osb.shape[2], odsa.shape[2]
    tm = min(tm, s)
    row = lambda w: pl.BlockSpec((1, tm, w), lambda b, i, j: (b, i, 0))
    full = lambda shape: pl.BlockSpec(shape, lambda b, i, j: (0,) * len(shape))
    return pl.pallas_call(
        functools.partial(_ffn_kernel, final=final),
        grid=(bsz, s // tm, dff // tf),
        in_specs=[row(d), row(sb_w), row(dsa_w),
                  pl.BlockSpec((1, 6, d), lambda b, i, j: (b, 0, 0)),
                  full((1, sb_w)), full((1, dsa_w)), full((sb_w + dsa_w, d)), full((1, d)),
                  pl.BlockSpec((d, tf), lambda b, i, j: (0, j)),
                  pl.BlockSpec((d, tf), lambda b, i, j: (0, j)),
                  pl.BlockSpec((tf, d), lambda b, i, j: (j, 0)),
                  full((1, d)),
                  pl.BlockSpec((1, 2, d), lambda b, i, j: (b, 0, 0))],
        out_specs=row(d),
        out_shape=jax.ShapeDtypeStruct((bsz, s, d), F32),
        scratch_shapes=[pltpu.VMEM((tm, d), F32), pltpu.VMEM((tm, d), BF16),
                        pltpu.VMEM((tm, d), F32)],
        compiler_params=pltpu.CompilerParams(
            dimension_semantics=("parallel", "parallel", "arbitrary"),
            vmem_limit_bytes=VMEM_LIMIT),
        name="out_ffn",
    )(x, osb, odsa, mod, gsb, gdsa, wo, gain, wg, wu, wd, fgain, fmod)


def kernel(x, c, w_mod, b_mod, norm1_gain, norm2_gain, w_in, kv_gain, w_uk, w_uv, sb_out_gain,
           dsa_out_gain, w_o, w_gate, w_up, w_down, w_mod_final, b_mod_final, final_gain):
    bsz, s, d = x.shape
    depth = w_in.shape[0]
    sb_w = sb_out_gain.shape[1]
    dsa_w = dsa_out_gain.shape[1]
    kv_w = kv_gain.shape[1]
    topk = min(TOPK_MAX, s // 4)
    assert s % max(SB_KC, DSA_KC) == 0 and topk % Q_BLOCK == 0
    assert HEAD_DIM * 2 == LANES and w_uk.shape[2] == HEAD_DIM and s // LANES <= 256

    mod = _mod(c, w_mod, b_mod).reshape(depth, bsz, 6, d)
    fmod = _mod(c, w_mod_final[None], b_mod_final[None]).reshape(bsz, 2, d)

    n_a = 3 * sb_w + dsa_w + kv_w
    n_idx = N_IDX_HEADS * IDX_DIM + IDX_DIM + N_IDX_HEADS
    idx_pad = N_IDX_HEADS * IDX_DIM + LANES - n_idx
    bkv = jnp.concatenate([jnp.zeros((3 * HEAD_DIM,), F32), jnp.ones((HEAD_DIM,), F32)])[None]

    for l in range(depth):
        wa = w_in[l][:, :n_a].astype(BF16)
        wi = jnp.pad(w_in[l][:, n_a:], ((0, 0), (0, idx_pad)))
        wih = wi.astype(BF16)
        wil = (wi - wih.astype(F32)).astype(BF16)
        wkv = jnp.concatenate([w_uk[l], w_uk[l], w_uv[l], jnp.zeros_like(w_uv[l])],
                              axis=1).astype(BF16)
        (sbq, sbk, sbv, dq, kda, qi2, ki1, wt, vd1t) = _in_proj(
            x, mod[l], norm1_gain[l][None], wa, wih, wil, kv_gain[l][None], wkv, bkv,
            sb_w=sb_w, dsa_w=dsa_w, kv_w=kv_w)
        osb = _sb_attention(sbq, sbk, sbv)
        odsa = _dsa_attention(dq, kda, vd1t, qi2, ki1, wt, topk=topk)
        x = _out_ffn(x, osb, odsa, mod[l], sb_out_gain[l][None], dsa_out_gain[l][None],
                     w_o[l].astype(BF16), norm2_gain[l][None], w_gate[l].astype(BF16),
                     w_up[l].astype(BF16), w_down[l].astype(BF16), final_gain[None], fmod,
                     final=(l == depth - 1))
    return x
```

```python
import functools

import jax
import jax.numpy as jnp
from jax import lax
from jax.experimental import pallas as pl
from jax.experimental.pallas import tpu as pltpu

F32 = jnp.float32
BF16 = jnp.bfloat16
I32 = jnp.int32

HEAD_DIM = 64
LANES = 128
Q_BLOCK = 128
N_IDX_HEADS = 4
IDX_DIM = 64
TOPK_MAX = 256
RMS_EPS = 1e-6
INT_MIN = -2 ** 31
NEG_BIG = -1e30
VMEM_LIMIT = 52 * 1024 * 1024
SB_KC = 512
SB_STEP = 2
SB_ROW_TILE = 64
SB_UNDERFLOW = 110.0
DSA_KC = 512


def _nt(a, b):
    return lax.dot_general(a, b, (((1,), (1,)), ((), ())), preferred_element_type=F32)


def _dot(a, b):
    return jnp.dot(a, b, preferred_element_type=F32)


def _split(x):
    hi = x.astype(BF16)
    lo = (x - hi.astype(F32)).astype(BF16)
    return hi, lo


def _rms(x, gain):
    ms = jnp.mean(x * x, axis=-1, keepdims=True)
    return x * lax.rsqrt(ms + RMS_EPS) * gain


def _mod_kernel(c_ref, w_ref, b_ref, o_ref):
    c = c_ref[...]
    ca = c / (1.0 + jnp.exp(-c))
    ch, cl = _split(ca)
    wh, wl = _split(w_ref[0])
    o_ref[0] = _dot(ch, wh) + _dot(ch, wl) + _dot(cl, wh) + b_ref[0]


def _mod(c, w, b, tn=1024):
    nl, d, n = w.shape
    bsz = c.shape[0]
    return pl.pallas_call(
        _mod_kernel,
        grid=(nl, n // tn),
        in_specs=[pl.BlockSpec((bsz, d), lambda l, j: (0, 0)),
                  pl.BlockSpec((1, d, tn), lambda l, j: (l, 0, j)),
                  pl.BlockSpec((1, 1, tn), lambda l, j: (l, 0, j))],
        out_specs=pl.BlockSpec((1, bsz, tn), lambda l, j: (l, 0, j)),
        out_shape=jax.ShapeDtypeStruct((nl, bsz, n), F32),
        name="adaln_mod",
    )(c, w, b.reshape(nl, 1, n))


def _in_kernel(x_ref, mod_ref, g_ref, wa_ref, wih_ref, wil_ref, kvg_ref, wkv_ref, bkv_ref,
               sbq_ref, sbk_ref, sbv_ref, dq_ref, kda_ref, qi2_ref, ki1_ref, wt_ref, vd1t_ref,
               *, sb_w, dsa_w, kv_w):
    tm = x_ref.shape[1]
    x = x_ref[0]
    mod = mod_ref[0]
    h = _rms(x, g_ref[...]) * (1.0 + mod[1:2]) + mod[0:1]
    hh, hl = _split(h)
    main = _dot(hh, wa_ref[...])
    scale = HEAD_DIM ** -0.5
    sbq_ref[0] = (main[:, 0:sb_w] * scale).astype(BF16)
    sbk_ref[0] = main[:, sb_w:2 * sb_w].astype(BF16)
    sbv_ref[0] = main[:, 2 * sb_w:3 * sb_w].astype(BF16)
    o = 3 * sb_w
    dq_ref[0] = (main[:, o:o + dsa_w] * scale).astype(BF16)
    o += dsa_w
    kv = _rms(main[:, o:o + kv_w], kvg_ref[...])
    kvp = _dot(kv.astype(BF16), wkv_ref[...]) + bkv_ref[...]
    lane = lax.broadcasted_iota(I32, (tm, LANES), 1)
    pos = pl.program_id(1) * tm + lax.broadcasted_iota(I32, (tm, LANES), 0)
    posf = jnp.where(lane == 0, pos >> 7, jnp.where(lane == 1, pos & (LANES - 1), 0)).astype(F32)
    kda_ref[0, :, 0:LANES] = kvp[:, 0:LANES].astype(BF16)
    kda_ref[0, :, LANES:2 * LANES] = posf.astype(BF16)
    vd1t_ref[0] = kvp[:, LANES:2 * LANES].T.astype(BF16)
    wih = wih_ref[...]
    idx = _dot(hh, wih) + _dot(hh, wil_ref[...]) + _dot(hl, wih)
    first = lane < IDX_DIM
    for pr in range(N_IDX_HEADS // 2):
        pair = idx[:, pr * LANES:(pr + 1) * LANES] * (IDX_DIM ** -0.5)
        rolled = pltpu.roll(pair, IDX_DIM, 1)
        for half in range(2):
            dup = jnp.where(first, pair, rolled) if half == 0 else jnp.where(first, rolled, pair)
            qh, ql = _split(dup)
            hd = 2 * pr + half
            qi2_ref[0, :, 2 * hd * LANES:(2 * hd + 1) * LANES] = qh
            qi2_ref[0, :, (2 * hd + 1) * LANES:(2 * hd + 2) * LANES] = jnp.where(
                first, ql, jnp.zeros_like(ql))
    nq = N_IDX_HEADS * IDX_DIM
    kw = idx[:, nq:nq + LANES]
    kh, kl = _split(jnp.where(first, kw, pltpu.roll(kw, IDX_DIM, 1)))
    ki1_ref[0] = jnp.where(first, kh, kl)
    kwt = kw.T
    wt_ref[0] = kwt[IDX_DIM:IDX_DIM + 8, :] * (N_IDX_HEADS ** -0.5)


def _in_proj(x, mod, gain, wa, wih, wil, kvg, wkv, bkv, *, sb_w, dsa_w, kv_w, tm=512):
    bsz, s, d = x.shape
    na = wa.shape[1]
    ni = wih.shape[1]
    nkv = wkv.shape[1]
    full = lambda shape: pl.BlockSpec(shape, lambda b, i: (0,) * len(shape))
    row = lambda w: pl.BlockSpec((1, tm, w), lambda b, i: (b, i, 0))
    outs = [
        (sb_w, BF16), (sb_w, BF16), (sb_w, BF16), (dsa_w, BF16),
        (2 * LANES, BF16), (2 * LANES * N_IDX_HEADS, BF16), (LANES, BF16),
    ]
    out_shape = [jax.ShapeDtypeStruct((bsz, s, w), dt) for w, dt in outs]
    out_specs = [row(w) for w, _ in outs]
    for rows, dt in ((8, F32), (LANES, BF16)):
        out_shape.append(jax.ShapeDtypeStruct((bsz, rows, s), dt))
        out_specs.append(pl.BlockSpec((1, rows, tm), lambda b, i: (b, 0, i)))
    return pl.pallas_call(
        functools.partial(_in_kernel, sb_w=sb_w, dsa_w=dsa_w, kv_w=kv_w),
        grid=(bsz, s // tm),
        in_specs=[row(d),
                  pl.BlockSpec((1, 6, d), lambda b, i: (b, 0, 0)),
                  full((1, d)), full((d, na)), full((d, ni)), full((d, ni)),
                  full((1, kv_w)), full((kv_w, nkv)), full((1, nkv))],
        out_specs=out_specs,
        out_shape=out_shape,
        compiler_params=pltpu.CompilerParams(
            dimension_semantics=("parallel", "parallel"), vmem_limit_bytes=VMEM_LIMIT),
        name="in_proj",
    )(x, mod, gain, wa, wih, wil, kvg, wkv, bkv)


def _sb_kernel(q_ref, k_ref, v_ref, o_ref, car_ref, *, npair):
    i = pl.program_id(1)
    t = Q_BLOCK
    kc = SB_KC
    nsub = kc // LANES
    lane = lax.broadcasted_iota(I32, (t, LANES), 1)
    first = lane < HEAD_DIM
    zb = jnp.zeros((t, LANES), BF16)
    r2 = lax.broadcasted_iota(I32, (LANES, 2 * LANES), 0)
    c2 = lax.broadcasted_iota(I32, (LANES, 2 * LANES), 1)
    uo = jnp.where((c2 >= LANES) | (r2 > c2), 1.0, 0.0).astype(BF16)
    ct = i // nsub
    rt = SB_ROW_TILE

    def stage_scores(pr, start, ns):
        cols = slice(pr * LANES, (pr + 1) * LANES)
        q2 = q_ref[0, :, cols]
        qs = jnp.concatenate([jnp.where(first, q2, zb), jnp.where(first, zb, q2)], axis=0)
        z = _nt(qs, k_ref[0, pl.ds(start, ns * LANES), cols])
        return jnp.concatenate([z[:, j * LANES:(j + 1) * LANES] for j in range(ns)], axis=0)

    def causal_tile(r0):
        rr = (r0 + lax.broadcasted_iota(I32, (rt, LANES), 0)) & (t - 1)
        return lax.broadcasted_iota(I32, (rt, LANES), 1) < rr

    def stage_cumsum(u, top, ns):
        ds, sps = [], []
        for r0 in range(0, ns * 2 * t, rt):
            ut = u[r0:r0 + rt]
            sp = jnp.maximum(ut, 0.0) + jnp.log(1.0 + jnp.exp(-jnp.abs(ut)))
            ds.append(ut - sp)
            if top and r0 >= (ns - 1) * 2 * t:
                sp = jnp.where(causal_tile(r0), sp, 0.0)
            sps.append(sp.astype(BF16))
        spb = jnp.concatenate(sps, axis=0)
        half = ns * t
        cs = jnp.concatenate([_dot(spb[:half], uo), _dot(spb[half:], uo)], axis=0)
        return ds, cs

    def stage_output(pr, start, ds, cs, top, ns):
        cols = slice(pr * LANES, (pr + 1) * LANES)
        v2 = v_ref[0, pl.ds(start, ns * LANES), cols]
        firstk = lax.broadcasted_iota(I32, (ns * LANES, LANES), 1) < HEAD_DIM
        zk = jnp.zeros((ns * LANES, LANES), BF16)
        ntile = 2 * t // rt
        runs = [jnp.zeros((rt, LANES), F32) if top else car_ref[pr, k * rt:(k + 1) * rt]
                for k in range(ntile)]
        pieces = [[None] * ntile for _ in range(ns)]
        for j in reversed(range(ns)):
            for k in range(ntile):
                r0 = j * 2 * t + k * rt
                a = jnp.exp(ds[r0 // rt] - (cs[r0:r0 + rt, :LANES] + runs[k]))
                if top and j == ns - 1:
                    a = jnp.where(causal_tile(r0), a, 0.0)
                pieces[j][k] = a.astype(BF16)
                runs[k] = runs[k] + cs[r0:r0 + rt, LANES:]
        for k in range(ntile):
            car_ref[pr, k * rt:(k + 1) * rt] = runs[k]
        pieces = [jnp.concatenate(p, axis=0) for p in pieces]
        a_e = jnp.concatenate([p[:t] for p in pieces], axis=1)
        a_o = jnp.concatenate([p[t:] for p in pieces], axis=1)
        pv = _dot(a_e, jnp.where(firstk, v2, zk)) + _dot(a_o, jnp.where(firstk, zk, v2))
        if top:
            o_ref[0, :, cols] = pv
        else:
            o_ref[0, :, cols] += pv
        return functools.reduce(jnp.minimum, runs)

    def all_pairs(start, top, ns):
        us, mids, mins = {}, {}, []
        for step in range(npair + 2):
            if step < npair:
                us[step] = stage_scores(step, start, ns)
            if 0 <= step - 1 < npair:
                mids[step - 1] = stage_cumsum(us.pop(step - 1), top, ns)
            if 0 <= step - 2 < npair:
                mins.append(stage_output(step - 2, start, *mids.pop(step - 2), top, ns))
        return functools.reduce(jnp.minimum, mins)

    top_start = pl.multiple_of(ct * kc, kc)
    for r in range(nsub):
        @pl.when(i % nsub == r)
        def _(r=r):
            all_pairs(top_start, True, r + 1)

    step_keys = SB_STEP * LANES
    n_steps = ct * (kc // step_keys)

    def cond(state):
        j, min_carry = state
        return (j < n_steps) & (min_carry < SB_UNDERFLOW)

    def body(state):
        j, _ = state
        start = pl.multiple_of(top_start - (j + 1) * step_keys, step_keys)
        return j + 1, jnp.min(all_pairs(start, False, SB_STEP))

    lax.while_loop(cond, body, (jnp.int32(0), jnp.min(car_ref[...])))


def _sb_attention(q, k, v):
    bsz, s, w = q.shape
    npair = w // LANES
    nq = s // Q_BLOCK
    return pl.pallas_call(
        functools.partial(_sb_kernel, npair=npair),
        grid=(bsz, nq),
        in_specs=[pl.BlockSpec((1, Q_BLOCK, w), lambda b, i: (b, i, 0)),
                  pl.BlockSpec((1, s, w), lambda b, i: (b, 0, 0)),
                  pl.BlockSpec((1, s, w), lambda b, i: (b, 0, 0))],
        out_specs=pl.BlockSpec((1, Q_BLOCK, w), lambda b, i: (b, i, 0)),
        out_shape=jax.ShapeDtypeStruct((bsz, s, w), F32),
        scratch_shapes=[pltpu.VMEM((npair, 2 * Q_BLOCK, LANES), F32)],
        compiler_params=pltpu.CompilerParams(
            dimension_semantics=("parallel", "arbitrary"), vmem_limit_bytes=VMEM_LIMIT),
        name="sb_attention",
    )(q, k, v)


def _key_to_float(k):
    return lax.bitcast_convert_type(k ^ ((k >> 31) & 0x7FFFFFFF), F32)


def _dsa_kernel(dq_ref, kda_ref, vd1t_ref, qi2_ref, ki1_ref, wt_ref, o_ref,
                key_ref, qa_ref, acc_ref, m_ref, thr_ref, sc_ref, *, topk, n_heads):
    i = pl.program_id(1)
    t = Q_BLOCK
    kc = DSA_KC
    nsub = kc // t
    n_full = (i + 1) // nsub
    rem = (i + 1) % nsub
    n_kc = n_full + jnp.minimum(rem, 1)
    kf = float(topk)
    lane = lax.broadcasted_iota(I32, (t, LANES), 1)
    row = lax.broadcasted_iota(I32, (t, LANES), 0)
    first = lane < HEAD_DIM

    zb = jnp.zeros((t, LANES), BF16)
    for hd in range(n_heads):
        q2 = dq_ref[0, :, (hd // 2) * LANES:(hd // 2 + 1) * LANES]
        slope = 2.0 ** (-8.0 * (hd + 1) / n_heads)
        aug = jnp.where(lane == 0, LANES * slope, jnp.where(lane == 1, slope, 0.0)).astype(BF16)
        qa_ref[hd * t:(hd + 1) * t, 0:LANES] = (
            jnp.where(first, q2, zb) if hd % 2 == 0 else jnp.where(first, zb, q2))
        qa_ref[hd * t:(hd + 1) * t, LANES:2 * LANES] = aug

    wt = wt_ref[0]
    rowlane = (lax.broadcasted_iota(I32, (kc, LANES), 0)
               - lax.broadcasted_iota(I32, (kc, LANES), 1))

    def score_rows(start, rows):
        sc_ref[pl.ds(start, rows), :] = _nt(kda_ref[0, pl.ds(start, rows), :], qa_ref[...])
        k1 = ki1_ref[0, pl.ds(start, rows), :]
        lhs = jnp.concatenate([k1, k1], axis=1)
        score = None
        for pr in range(N_IDX_HEADS // 2):
            w2 = jnp.concatenate(
                [qi2_ref[0, :, 2 * hd * LANES:(2 * hd + 2) * LANES] for hd in (2 * pr, 2 * pr + 1)],
                axis=0)
            lg = _nt(lhs, w2)
            for half in range(2):
                hd = 2 * pr + half
                term = jnp.maximum(lg[:, half * LANES:(half + 1) * LANES], 0.0) * wt[hd:hd + 1, :]
                score = term if score is None else score + term
        key_ref[pl.ds(start, rows), :] = jnp.where(rowlane[:rows] > i * t - start, -jnp.inf, score)

    def score_chunk(c, carry):
        score_rows(pl.multiple_of(c * kc, kc), kc)
        return carry

    lax.fori_loop(0, n_full, score_chunk, 0)
    rem_start = pl.multiple_of(n_full * kc, kc)
    for r in range(1, nsub):
        @pl.when(rem == r)
        def _(r=r):
            score_rows(rem_start, r * t)
            key_ref[pl.ds(rem_start + r * t, kc - r * t), :] = jnp.full(
                (kc - r * t, LANES), -jnp.inf, F32)

    def count(pred):
        def body(c, acc):
            start = pl.multiple_of(c * kc, kc)
            m = jnp.where(pred(key_ref[pl.ds(start, kc), :]), 1.0, 0.0)
            m = m.reshape(8, kc // 8, LANES).sum(axis=0)
            return acc + m.reshape(kc // 64, 8, LANES).sum(axis=0)
        part = lax.fori_loop(0, n_kc, body, jnp.zeros((8, LANES), F32))
        return jnp.sum(part, axis=0, keepdims=True)

    searching = (i + 1) * t > topk

    def search(n128):
        pieces = [(lo_r, min(lo_r + kc, n128 * t)) for lo_r in range(0, n128 * t, kc)]

        def bit_step(bi, lo):
            cand = lo + lax.shift_left(jnp.int32(1), 31 - bi)
            cand_f = _key_to_float(cand)
            acc = None
            for lo_r, hi_r in pieces:
                m = jnp.where(key_ref[lo_r:hi_r, :] >= cand_f, 1.0, 0.0)
                m = m.reshape(8, (hi_r - lo_r) // 8, LANES).sum(axis=0)
                m = m.reshape((hi_r - lo_r) // 64, 8, LANES).sum(axis=0)
                acc = m if acc is None else acc + m
            cnt = jnp.sum(acc, axis=0, keepdims=True)
            return jnp.where(cnt >= kf, cand, lo)

        return _key_to_float(lax.fori_loop(0, 32, bit_step, jnp.full((1, LANES), INT_MIN, I32)))

    thr_ref[...] = jnp.full(thr_ref.shape, jnp.finfo(F32).min, F32)
    for n128 in range(topk // t + 1, key_ref.shape[0] // t + 1):
        @pl.when(i + 1 == n128)
        def _(n128=n128):
            thr_ref[...] = jnp.broadcast_to(search(n128), thr_ref.shape)

    thr = thr_ref[0:1, :]
    cnt_ge = count(lambda blk: blk >= thr)

    @pl.when((jnp.max(cnt_ge) > kf) & searching)
    def _():
        need = kf - count(lambda blk: blk > thr)
        sl = jnp.where(lane < row, 1.0, 0.0).astype(BF16)
        nsub = kc // t

        def body(c, before):
            start = pl.multiple_of(c * kc, kc)
            subs = [key_ref[pl.ds(start + j * t, t), :] for j in range(nsub)]
            eqfs = [jnp.where(sub == thr, 1.0, 0.0) for sub in subs]
            ranks = [_dot(sl, eqf.astype(BF16)) for eqf in eqfs]
            for j in range(nsub):
                demote = (subs[j] == thr) & (ranks[j] + before >= need)
                key_ref[pl.ds(start + j * t, t), :] = jnp.where(demote, -jnp.inf, subs[j])
                before = before + jnp.sum(eqfs[j], axis=0, keepdims=True)
            return before

        lax.fori_loop(0, n_kc, body, jnp.zeros((1, LANES), F32))

    m_ref[...] = jnp.full(m_ref.shape, NEG_BIG, F32)
    acc_ref[...] = jnp.zeros(acc_ref.shape, F32)

    def attn_rows(start, rows):
        vt = vd1t_ref[0, :, pl.ds(start, rows)]
        bias = jnp.where(key_ref[pl.ds(start, rows), :] >= thr, 0.0, -jnp.inf)
        sc = sc_ref[pl.ds(start, rows), :] + jnp.concatenate([bias] * n_heads, axis=1)
        m_old = m_ref[...]
        m_new = jnp.maximum(m_old, jnp.max(sc, axis=0, keepdims=True))
        acc_ref[...] = (acc_ref[...] * jnp.exp(m_old - m_new)
                        + _dot(vt, jnp.exp(sc - m_new).astype(BF16)))
        m_ref[...] = m_new

    def attn_chunk(c, carry):
        attn_rows(pl.multiple_of(c * kc, kc), kc)
        return carry

    lax.fori_loop(0, n_full, attn_chunk, 0)
    for r in range(1, nsub):
        @pl.when(rem == r)
        def _(r=r):
            attn_rows(rem_start, r * t)

    for pr in range(n_heads // 2):
        a_e = acc_ref[:, (2 * pr) * t:(2 * pr + 1) * t]
        a_o = acc_ref[:, (2 * pr + 1) * t:(2 * pr + 2) * t]
        r = jnp.concatenate([a_e[:HEAD_DIM] / a_e[HEAD_DIM:], a_o[:HEAD_DIM] / a_o[HEAD_DIM:]], axis=0)
        o_ref[0, :, pr * LANES:(pr + 1) * LANES] = r.T


def _dsa_attention(dq, kda, vd1t, qi2, ki1, wt, *, topk):
    bsz, s, w = dq.shape
    nq = s // Q_BLOCK
    n_heads = w // HEAD_DIM
    blk = lambda wd: pl.BlockSpec((1, Q_BLOCK, wd), lambda b, i: (b, i, 0))
    full = lambda wd: pl.BlockSpec((1, s, wd), lambda b, i: (b, 0, 0))
    return pl.pallas_call(
        functools.partial(_dsa_kernel, topk=topk, n_heads=n_heads),
        grid=(bsz, nq),
        in_specs=[blk(w), full(2 * LANES),
                  pl.BlockSpec((1, LANES, s), lambda b, i: (b, 0, 0)),
                  blk(qi2.shape[2]), full(LANES),
                  pl.BlockSpec((1, 8, Q_BLOCK), lambda b, i: (b, 0, i))],
        out_specs=blk(w),
        out_shape=jax.ShapeDtypeStruct((bsz, s, w), F32),
        scratch_shapes=[pltpu.VMEM((s, LANES), F32),
                        pltpu.VMEM((n_heads * Q_BLOCK, 2 * LANES), BF16),
                        pltpu.VMEM((LANES, n_heads * Q_BLOCK), F32),
                        pltpu.VMEM((1, n_heads * Q_BLOCK), F32),
                        pltpu.VMEM((8, LANES), F32),
                        pltpu.VMEM((s, n_heads * Q_BLOCK), F32)],
        compiler_params=pltpu.CompilerParams(
            dimension_semantics=("parallel", "arbitrary"), vmem_limit_bytes=VMEM_LIMIT),
        name="dsa_attention",
    )(dq, kda, vd1t, qi2, ki1, wt)


def _ffn_kernel(x_ref, osb_ref, odsa_ref, mod_ref, gsb_ref, gdsa_ref, wo_ref, g_ref,
                wg_ref, wu_ref, wd_ref, fg_ref, fmod_ref, o_ref,
                x1_ref, h_ref, acc_ref, *, final):
    j = pl.program_id(2)

    @pl.when(j == 0)
    def _():
        mod = mod_ref[0]
        sb_w = osb_ref.shape[2]
        a = _rms(osb_ref[0], gsb_ref[...]).astype(BF16)
        b = _rms(odsa_ref[0], gdsa_ref[...]).astype(BF16)
        y = _dot(a, wo_ref[0:sb_w, :]) + _dot(b, wo_ref[sb_w:, :])
        x1 = x_ref[0] + mod[2:3] * y
        x1_ref[...] = x1
        h = _rms(x1, g_ref[...]) * (1.0 + mod[4:5]) + mod[3:4]
        h_ref[...] = h.astype(BF16)

    h = h_ref[...]
    g = _dot(h, wg_ref[...])
    u = _dot(h, wu_ref[...])
    act = (g / (1.0 + jnp.exp(-g))) * u
    part = _dot(act.astype(BF16), wd_ref[...])

    @pl.when(j == 0)
    def _():
        acc_ref[...] = part

    @pl.when(j > 0)
    def _():
        acc_ref[...] += part

    @pl.when(j == pl.num_programs(2) - 1)
    def _():
        y = x1_ref[...] + mod_ref[0][5:6] * acc_ref[...]
        if final:
            fmod = fmod_ref[0]
            y = _rms(y, fg_ref[...]) * (1.0 + fmod[1:2]) + fmod[0:1]
        o_ref[0] = y


def _out_ffn(x, osb, odsa, mod, gsb, gdsa, wo, gain, wg, wu, wd, fgain, fmod, *, final,
             tm=512, tf=1408):
    bsz, s, d = x.shape
    dff = wg.shape[1]
    sb_w, dsa_w = osb.shape[2], odsa.shape[2]
    tm = min(tm, s)
    row = lambda w: pl.BlockSpec((1, tm, w), lambda b, i, j: (b, i, 0))
    full = lambda shape: pl.BlockSpec(shape, lambda b, i, j: (0,) * len(shape))
    return pl.pallas_call(
        functools.partial(_ffn_kernel, final=final),
        grid=(bsz, s // tm, dff // tf),
        in_specs=[row(d), row(sb_w), row(dsa_w),
                  pl.BlockSpec((1, 6, d), lambda b, i, j: (b, 0, 0)),
                  full((1, sb_w)), full((1, dsa_w)), full((sb_w + dsa_w, d)), full((1, d)),
                  pl.BlockSpec((d, tf), lambda b, i, j: (0, j)),
                  pl.BlockSpec((d, tf), lambda b, i, j: (0, j)),
                  pl.BlockSpec((tf, d), lambda b, i, j: (j, 0)),
                  full((1, d)),
                  pl.BlockSpec((1, 2, d), lambda b, i, j: (b, 0, 0))],
        out_specs=row(d),
        out_shape=jax.ShapeDtypeStruct((bsz, s, d), F32),
        scratch_shapes=[pltpu.VMEM((tm, d), F32), pltpu.VMEM((tm, d), BF16),
                        pltpu.VMEM((tm, d), F32)],
        compiler_params=pltpu.CompilerParams(
            dimension_semantics=("parallel", "parallel", "arbitrary"),
            vmem_limit_bytes=VMEM_LIMIT),
        name="out_ffn",
    )(x, osb, odsa, mod, gsb, gdsa, wo, gain, wg, wu, wd, fgain, fmod)


def kernel(x, c, w_mod, b_mod, norm1_gain, norm2_gain, w_in, kv_gain, w_uk, w_uv, sb_out_gain,
           dsa_out_gain, w_o, w_gate, w_up, w_down, w_mod_final, b_mod_final, final_gain):
    bsz, s, d = x.shape
    depth = w_in.shape[0]
    sb_w = sb_out_gain.shape[1]
    dsa_w = dsa_out_gain.shape[1]
    kv_w = kv_gain.shape[1]
    topk = min(TOPK_MAX, s // 4)
    assert s % max(SB_KC, DSA_KC) == 0 and topk % Q_BLOCK == 0
    assert HEAD_DIM * 2 == LANES and w_uk.shape[2] == HEAD_DIM and s // LANES <= 256

    mod = _mod(c, w_mod, b_mod).reshape(depth, bsz, 6, d)
    fmod = _mod(c, w_mod_final[None], b_mod_final[None]).reshape(bsz, 2, d)

    n_a = 3 * sb_w + dsa_w + kv_w
    n_idx = N_IDX_HEADS * IDX_DIM + IDX_DIM + N_IDX_HEADS
    idx_pad = N_IDX_HEADS * IDX_DIM + LANES - n_idx
    bkv = jnp.concatenate([jnp.zeros((3 * HEAD_DIM,), F32), jnp.ones((HEAD_DIM,), F32)])[None]

    for l in range(depth):
        wa = w_in[l][:, :n_a].astype(BF16)
        wi = jnp.pad(w_in[l][:, n_a:], ((0, 0), (0, idx_pad)))
        wih = wi.astype(BF16)
        wil = (wi - wih.astype(F32)).astype(BF16)
        wkv = jnp.concatenate([w_uk[l], w_uk[l], w_uv[l], jnp.zeros_like(w_uv[l])],
                              axis=1).astype(BF16)
        (sbq, sbk, sbv, dq, kda, qi2, ki1, wt, vd1t) = _in_proj(
            x, mod[l], norm1_gain[l][None], wa, wih, wil, kv_gain[l][None], wkv, bkv,
            sb_w=sb_w, dsa_w=dsa_w, kv_w=kv_w)
        osb = _sb_attention(sbq, sbk, sbv)
        odsa = _dsa_attention(dq, kda, vd1t, qi2, ki1, wt, topk=topk)
        x = _out_ffn(x, osb, odsa, mod[l], sb_out_gain[l][None], dsa_out_gain[l][None],
                     w_o[l].astype(BF16), norm2_gain[l][None], w_gate[l].astype(BF16),
                     w_up[l].astype(BF16), w_down[l].astype(BF16), final_gain[None], fmod,
                     final=(l == depth - 1))
    return x
```

```python
import functools

import jax
import jax.numpy as jnp
from jax import lax
from jax.experimental import pallas as pl
from jax.experimental.pallas import tpu as pltpu

F32 = jnp.float32
BF16 = jnp.bfloat16
I32 = jnp.int32

HEAD_DIM = 64
LANES = 128
Q_BLOCK = 128
N_IDX_HEADS = 4
IDX_DIM = 64
TOPK_MAX = 256
RMS_EPS = 1e-6
INT_MIN = -2 ** 31
NEG_BIG = -1e30
VMEM_LIMIT = 52 * 1024 * 1024
SB_KC = 512
SB_STEP = 2
SB_ROW_TILE = 64
SB_UNDERFLOW = 110.0
DSA_KC = 512


def _nt(a, b):
    return lax.dot_general(a, b, (((1,), (1,)), ((), ())), preferred_element_type=F32)


def _dot(a, b):
    return jnp.dot(a, b, preferred_element_type=F32)


def _split(x):
    hi = x.astype(BF16)
    lo = (x - hi.astype(F32)).astype(BF16)
    return hi, lo


def _rms(x, gain):
    ms = jnp.mean(x * x, axis=-1, keepdims=True)
    return x * lax.rsqrt(ms + RMS_EPS) * gain


def _mod_kernel(c_ref, w_ref, b_ref, o_ref):
    c = c_ref[...]
    ca = c / (1.0 + jnp.exp(-c))
    ch, cl = _split(ca)
    wh, wl = _split(w_ref[0])
    o_ref[0] = _dot(ch, wh) + _dot(ch, wl) + _dot(cl, wh) + b_ref[0]


def _mod(c, w, b, tn=1024):
    nl, d, n = w.shape
    bsz = c.shape[0]
    return pl.pallas_call(
        _mod_kernel,
        grid=(nl, n // tn),
        in_specs=[pl.BlockSpec((bsz, d), lambda l, j: (0, 0)),
                  pl.BlockSpec((1, d, tn), lambda l, j: (l, 0, j)),
                  pl.BlockSpec((1, 1, tn), lambda l, j: (l, 0, j))],
        out_specs=pl.BlockSpec((1, bsz, tn), lambda l, j: (l, 0, j)),
        out_shape=jax.ShapeDtypeStruct((nl, bsz, n), F32),
        name="adaln_mod",
    )(c, w, b.reshape(nl, 1, n))


def _in_kernel(x_ref, mod_ref, g_ref, wa_ref, wih_ref, wil_ref, kvg_ref, wkv_ref, bkv_ref,
               sbq_ref, sbk_ref, sbv_ref, dq_ref, kda_ref, qi2_ref, ki1_ref, wt_ref, vd1t_ref,
               *, sb_w, dsa_w, kv_w):
    tm = x_ref.shape[1]
    x = x_ref[0]
    mod = mod_ref[0]
    h = _rms(x, g_ref[...]) * (1.0 + mod[1:2]) + mod[0:1]
    hh, hl = _split(h)
    main = _dot(hh, wa_ref[...])
    scale = HEAD_DIM ** -0.5
    sbq_ref[0] = (main[:, 0:sb_w] * scale).astype(BF16)
    sbk_ref[0] = main[:, sb_w:2 * sb_w].astype(BF16)
    sbv_ref[0] = main[:, 2 * sb_w:3 * sb_w].astype(BF16)
    o = 3 * sb_w
    dq_ref[0] = (main[:, o:o + dsa_w] * scale).astype(BF16)
    o += dsa_w
    kv = _rms(main[:, o:o + kv_w], kvg_ref[...])
    kvp = _dot(kv.astype(BF16), wkv_ref[...]) + bkv_ref[...]
    lane = lax.broadcasted_iota(I32, (tm, LANES), 1)
    pos = pl.program_id(1) * tm + lax.broadcasted_iota(I32, (tm, LANES), 0)
    posf = jnp.where(lane == 0, pos >> 7, jnp.where(lane == 1, pos & (LANES - 1), 0)).astype(F32)
    kda_ref[0, :, 0:LANES] = kvp[:, 0:LANES].astype(BF16)
    kda_ref[0, :, LANES:2 * LANES] = posf.astype(BF16)
    vd1t_ref[0] = kvp[:, LANES:2 * LANES].T.astype(BF16)
    wih = wih_ref[...]
    idx = _dot(hh, wih) + _dot(hh, wil_ref[...]) + _dot(hl, wih)
    first = lane < IDX_DIM
    for pr in range(N_IDX_HEADS // 2):
        pair = idx[:, pr * LANES:(pr + 1) * LANES] * (IDX_DIM ** -0.5)
        rolled = pltpu.roll(pair, IDX_DIM, 1)
        for half in range(2):
            dup = jnp.where(first, pair, rolled) if half == 0 else jnp.where(first, rolled, pair)
            qh, ql = _split(dup)
            hd = 2 * pr + half
            qi2_ref[0, :, 2 * hd * LANES:(2 * hd + 1) * LANES] = qh
            qi2_ref[0, :, (2 * hd + 1) * LANES:(2 * hd + 2) * LANES] = jnp.where(
                first, ql, jnp.zeros_like(ql))
    nq = N_IDX_HEADS * IDX_DIM
    kw = idx[:, nq:nq + LANES]
    kh, kl = _split(jnp.where(first, kw, pltpu.roll(kw, IDX_DIM, 1)))
    ki1_ref[0] = jnp.where(first, kh, kl)
    kwt = kw.T
    wt_ref[0] = kwt[IDX_DIM:IDX_DIM + 8, :] * (N_IDX_HEADS ** -0.5)


def _in_proj(x, mod, gain, wa, wih, wil, kvg, wkv, bkv, *, sb_w, dsa_w, kv_w, tm=512):
    bsz, s, d = x.shape
    na = wa.shape[1]
    ni = wih.shape[1]
    nkv = wkv.shape[1]
    full = lambda shape: pl.BlockSpec(shape, lambda b, i: (0,) * len(shape))
    row = lambda w: pl.BlockSpec((1, tm, w), lambda b, i: (b, i, 0))
    outs = [
        (sb_w, BF16), (sb_w, BF16), (sb_w, BF16), (dsa_w, BF16),
        (2 * LANES, BF16), (2 * LANES * N_IDX_HEADS, BF16), (LANES, BF16),
    ]
    out_shape = [jax.ShapeDtypeStruct((bsz, s, w), dt) for w, dt in outs]
    out_specs = [row(w) for w, _ in outs]
    for rows, dt in ((8, F32), (LANES, BF16)):
        out_shape.append(jax.ShapeDtypeStruct((bsz, rows, s), dt))
        out_specs.append(pl.BlockSpec((1, rows, tm), lambda b, i: (b, 0, i)))
    return pl.pallas_call(
        functools.partial(_in_kernel, sb_w=sb_w, dsa_w=dsa_w, kv_w=kv_w),
        grid=(bsz, s // tm),
        in_specs=[row(d),
                  pl.BlockSpec((1, 6, d), lambda b, i: (b, 0, 0)),
                  full((1, d)), full((d, na)), full((d, ni)), full((d, ni)),
                  full((1, kv_w)), full((kv_w, nkv)), full((1, nkv))],
        out_specs=out_specs,
        out_shape=out_shape,
        compiler_params=pltpu.CompilerParams(
            dimension_semantics=("parallel", "parallel"), vmem_limit_bytes=VMEM_LIMIT),
        name="in_proj",
    )(x, mod, gain, wa, wih, wil, kvg, wkv, bkv)


def _sb_kernel(q_ref, k_ref, v_ref, o_ref, car_ref, *, npair):
    i = pl.program_id(1)
    t = Q_BLOCK
    kc = SB_KC
    nsub = kc // LANES
    lane = lax.broadcasted_iota(I32, (t, LANES), 1)
    first = lane < HEAD_DIM
    zb = jnp.zeros((t, LANES), BF16)
    r2 = lax.broadcasted_iota(I32, (LANES, 2 * LANES), 0)
    c2 = lax.broadcasted_iota(I32, (LANES, 2 * LANES), 1)
    uo = jnp.where((c2 >= LANES) | (r2 > c2), 1.0, 0.0).astype(BF16)
    ct = i // nsub
    rt = SB_ROW_TILE

    def stage_scores(pr, start, ns):
        cols = slice(pr * LANES, (pr + 1) * LANES)
        q2 = q_ref[0, :, cols]
        qs = jnp.concatenate([jnp.where(first, q2, zb), jnp.where(first, zb, q2)], axis=0)
        z = _nt(qs, k_ref[0, pl.ds(start, ns * LANES), cols])
        return jnp.concatenate([z[:, j * LANES:(j + 1) * LANES] for j in range(ns)], axis=0)

    def causal_tile(r0):
        rr = (r0 + lax.broadcasted_iota(I32, (rt, LANES), 0)) & (t - 1)
        return lax.broadcasted_iota(I32, (rt, LANES), 1) < rr

    def stage_cumsum(u, top, ns):
        ds, sps = [], []
        for r0 in range(0, ns * 2 * t, rt):
            ut = u[r0:r0 + rt]
            sp = jnp.maximum(ut, 0.0) + jnp.log(1.0 + jnp.exp(-jnp.abs(ut)))
            ds.append(ut - sp)
            if top and r0 >= (ns - 1) * 2 * t:
                sp = jnp.where(causal_tile(r0), sp, 0.0)
            sps.append(sp.astype(BF16))
        spb = jnp.concatenate(sps, axis=0)
        half = ns * t
        cs = jnp.concatenate([_dot(spb[:half], uo), _dot(spb[half:], uo)], axis=0)
        return ds, cs

    def stage_output(pr, start, ds, cs, top, ns):
        cols = slice(pr * LANES, (pr + 1) * LANES)
        v2 = v_ref[0, pl.ds(start, ns * LANES), cols]
        firstk = lax.broadcasted_iota(I32, (ns * LANES, LANES), 1) < HEAD_DIM
        zk = jnp.zeros((ns * LANES, LANES), BF16)
        ntile = 2 * t // rt
        runs = [jnp.zeros((rt, LANES), F32) if top else car_ref[pr, k * rt:(k + 1) * rt]
                for k in range(ntile)]
        pieces = [[None] * ntile for _ in range(ns)]
        for j in reversed(range(ns)):
            for k in range(ntile):
                r0 = j * 2 * t + k * rt
                a = jnp.exp(ds[r0 // rt] - (cs[r0:r0 + rt, :LANES] + runs[k]))
                if top and j == ns - 1:
                    a = jnp.where(causal_tile(r0), a, 0.0)
                pieces[j][k] = a.astype(BF16)
                runs[k] = runs[k] + cs[r0:r0 + rt, LANES:]
        for k in range(ntile):
            car_ref[pr, k * rt:(k + 1) * rt] = runs[k]
        pieces = [jnp.concatenate(p, axis=0) for p in pieces]
        a_e = jnp.concatenate([p[:t] for p in pieces], axis=1)
        a_o = jnp.concatenate([p[t:] for p in pieces], axis=1)
        pv = _dot(a_e, jnp.where(firstk, v2, zk)) + _dot(a_o, jnp.where(firstk, zk, v2))
        if top:
            o_ref[0, :, cols] = pv
        else:
            o_ref[0, :, cols] += pv
        return functools.reduce(jnp.minimum, runs)

    def all_pairs(start, top, ns):
        us, mids, mins = {}, {}, []
        for step in range(npair + 2):
            if step < npair:
                us[step] = stage_scores(step, start, ns)
            if 0 <= step - 1 < npair:
                mids[step - 1] = stage_cumsum(us.pop(step - 1), top, ns)
            if 0 <= step - 2 < npair:
                mins.append(stage_output(step - 2, start, *mids.pop(step - 2), top, ns))
        return functools.reduce(jnp.minimum, mins)

    top_start = pl.multiple_of(ct * kc, kc)
    for r in range(nsub):
        @pl.when(i % nsub == r)
        def _(r=r):
            all_pairs(top_start, True, r + 1)

    step_keys = SB_STEP * LANES
    n_steps = ct * (kc // step_keys)

    def cond(state):
        j, min_carry = state
        return (j < n_steps) & (min_carry < SB_UNDERFLOW)

    def body(state):
        j, _ = state
        start = pl.multiple_of(top_start - (j + 1) * step_keys, step_keys)
        return j + 1, jnp.min(all_pairs(start, False, SB_STEP))

    lax.while_loop(cond, body, (jnp.int32(0), jnp.min(car_ref[...])))


def _sb_attention(q, k, v):
    bsz, s, w = q.shape
    npair = w // LANES
    nq = s // Q_BLOCK
    return pl.pallas_call(
        functools.partial(_sb_kernel, npair=npair),
        grid=(bsz, nq),
        in_specs=[pl.BlockSpec((1, Q_BLOCK, w), lambda b, i: (b, i, 0)),
                  pl.BlockSpec((1, s, w), lambda b, i: (b, 0, 0)),
                  pl.BlockSpec((1, s, w), lambda b, i: (b, 0, 0))],
        out_specs=pl.BlockSpec((1, Q_BLOCK, w), lambda b, i: (b, i, 0)),
        out_shape=jax.ShapeDtypeStruct((bsz, s, w), F32),
        scratch_shapes=[pltpu.VMEM((npair, 2 * Q_BLOCK, LANES), F32)],
        compiler_params=pltpu.CompilerParams(
            dimension_semantics=("parallel", "arbitrary"), vmem_limit_bytes=VMEM_LIMIT),
        name="sb_attention",
    )(q, k, v)


def _key_to_float(k):
    return lax.bitcast_convert_type(k ^ ((k >> 31) & 0x7FFFFFFF), F32)


def _dsa_kernel(dq_ref, kda_ref, vd1t_ref, qi2_ref, ki1_ref, wt_ref, o_ref,
                key_ref, qa_ref, acc_ref, m_ref, thr_ref, sc_ref, *, topk, n_heads):
    i = pl.program_id(1)
    t = Q_BLOCK
    kc = DSA_KC
    nsub = kc // t
    n_full = (i + 1) // nsub
    rem = (i + 1) % nsub
    n_kc = n_full + jnp.minimum(rem, 1)
    kf = float(topk)
    lane = lax.broadcasted_iota(I32, (t, LANES), 1)
    row = lax.broadcasted_iota(I32, (t, LANES), 0)
    first = lane < HEAD_DIM

    zb = jnp.zeros((t, LANES), BF16)
    for hd in range(n_heads):
        q2 = dq_ref[0, :, (hd // 2) * LANES:(hd // 2 + 1) * LANES]
        slope = 2.0 ** (-8.0 * (hd + 1) / n_heads)
        aug = jnp.where(lane == 0, LANES * slope, jnp.where(lane == 1, slope, 0.0)).astype(BF16)
        qa_ref[hd * t:(hd + 1) * t, 0:LANES] = (
            jnp.where(first, q2, zb) if hd % 2 == 0 else jnp.where(first, zb, q2))
        qa_ref[hd * t:(hd + 1) * t, LANES:2 * LANES] = aug

    wt = wt_ref[0]
    rowlane = (lax.broadcasted_iota(I32, (kc, LANES), 0)
               - lax.broadcasted_iota(I32, (kc, LANES), 1))

    def score_rows(start, rows):
        sc_ref[pl.ds(start, rows), :] = _nt(kda_ref[0, pl.ds(start, rows), :], qa_ref[...])
        k1 = ki1_ref[0, pl.ds(start, rows), :]
        lhs = jnp.concatenate([k1, k1], axis=1)
        score = None
        for pr in range(N_IDX_HEADS // 2):
            w2 = jnp.concatenate(
                [qi2_ref[0, :, 2 * hd * LANES:(2 * hd + 2) * LANES] for hd in (2 * pr, 2 * pr + 1)],
                axis=0)
            lg = _nt(lhs, w2)
            for half in range(2):
                hd = 2 * pr + half
                term = jnp.maximum(lg[:, half * LANES:(half + 1) * LANES], 0.0) * wt[hd:hd + 1, :]
                score = term if score is None else score + term
        key_ref[pl.ds(start, rows), :] = jnp.where(rowlane[:rows] > i * t - start, -jnp.inf, score)

    def score_chunk(c, carry):
        score_rows(pl.multiple_of(c * kc, kc), kc)
        return carry

    lax.fori_loop(0, n_full, score_chunk, 0)
    rem_start = pl.multiple_of(n_full * kc, kc)
    for r in range(1, nsub):
        @pl.when(rem == r)
        def _(r=r):
            score_rows(rem_start, r * t)
            key_ref[pl.ds(rem_start + r * t, kc - r * t), :] = jnp.full(
                (kc - r * t, LANES), -jnp.inf, F32)

    searching = (i + 1) * t > topk

    def search(n128):
        pieces = [(lo_r, min(lo_r + kc, n128 * t)) for lo_r in range(0, n128 * t, kc)]

        def bit_step(bi, state):
            lo, cnt_lo = state
            cand = lo + lax.shift_left(jnp.int32(1), 31 - bi)
            cand_f = _key_to_float(cand)
            acc = None
            for lo_r, hi_r in pieces:
                m = jnp.where(key_ref[lo_r:hi_r, :] >= cand_f, 1.0, 0.0)
                m = m.reshape(8, (hi_r - lo_r) // 8, LANES).sum(axis=0)
                m = m.reshape((hi_r - lo_r) // 64, 8, LANES).sum(axis=0)
                acc = m if acc is None else acc + m
            cnt = jnp.sum(acc, axis=0, keepdims=True)
            take = cnt >= kf
            return jnp.where(take, cand, lo), jnp.where(take, cnt, cnt_lo)

        lo, cnt_lo = lax.fori_loop(0, 32, bit_step, (jnp.full((1, LANES), INT_MIN, I32),
                                                     jnp.full((1, LANES), kf, F32)))
        return _key_to_float(lo), cnt_lo

    thr_ref[...] = jnp.full(thr_ref.shape, jnp.finfo(F32).min, F32)
    for n128 in range(topk // t + 1, key_ref.shape[0] // t + 1):
        @pl.when(i + 1 == n128)
        def _(n128=n128):
            thr_f, cnt_f = search(n128)
            thr_ref[0:1, :] = thr_f
            thr_ref[1:2, :] = cnt_f

    thr = thr_ref[0:1, :]
    cnt_ge = thr_ref[1:2, :]

    @pl.when((jnp.max(cnt_ge) > kf) & searching)
    def _():
        excess = cnt_ge - kf
        su = jnp.where(lane > row, 1.0, 0.0).astype(BF16)

        def body(jj, after):
            start = pl.multiple_of((n_kc - 1 - jj) * kc, kc)
            subs = [key_ref[pl.ds(start + j * t, t), :] for j in range(nsub)]
            eqfs = [jnp.where(sub == thr, 1.0, 0.0) for sub in subs]
            ranks = [_dot(su, eqf.astype(BF16)) for eqf in eqfs]
            for j in reversed(range(nsub)):
                demote = (subs[j] == thr) & (ranks[j] + after < excess)
                key_ref[pl.ds(start + j * t, t), :] = jnp.where(demote, -jnp.inf, subs[j])
                after = after + jnp.sum(eqfs[j], axis=0, keepdims=True)
            return after

        lax.fori_loop(0, n_kc, body, jnp.zeros((1, LANES), F32))

    m_ref[...] = jnp.full(m_ref.shape, NEG_BIG, F32)
    acc_ref[...] = jnp.zeros(acc_ref.shape, F32)

    def attn_rows(start, rows):
        vt = vd1t_ref[0, :, pl.ds(start, rows)]
        bias = jnp.where(key_ref[pl.ds(start, rows), :] >= thr, 0.0, -jnp.inf)
        sc = sc_ref[pl.ds(start, rows), :] + jnp.concatenate([bias] * n_heads, axis=1)
        m_old = m_ref[...]
        m_new = jnp.maximum(m_old, jnp.max(sc, axis=0, keepdims=True))
        acc_ref[...] = (acc_ref[...] * jnp.exp(m_old - m_new)
                        + _dot(vt, jnp.exp(sc - m_new).astype(BF16)))
        m_ref[...] = m_new

    def attn_chunk(c, carry):
        attn_rows(pl.multiple_of(c * kc, kc), kc)
        return carry

    lax.fori_loop(0, n_full, attn_chunk, 0)
    for r in range(1, nsub):
        @pl.when(rem == r)
        def _(r=r):
            attn_rows(rem_start, r * t)

    for pr in range(n_heads // 2):
        a_e = acc_ref[:, (2 * pr) * t:(2 * pr + 1) * t]
        a_o = acc_ref[:, (2 * pr + 1) * t:(2 * pr + 2) * t]
        r = jnp.concatenate([a_e[:HEAD_DIM] / a_e[HEAD_DIM:], a_o[:HEAD_DIM] / a_o[HEAD_DIM:]], axis=0)
        o_ref[0, :, pr * LANES:(pr + 1) * LANES] = r.T


def _dsa_attention(dq, kda, vd1t, qi2, ki1, wt, *, topk):
    bsz, s, w = dq.shape
    nq = s // Q_BLOCK
    n_heads = w // HEAD_DIM
    blk = lambda wd: pl.BlockSpec((1, Q_BLOCK, wd), lambda b, i: (b, i, 0))
    full = lambda wd: pl.BlockSpec((1, s, wd), lambda b, i: (b, 0, 0))
    return pl.pallas_call(
        functools.partial(_dsa_kernel, topk=topk, n_heads=n_heads),
        grid=(bsz, nq),
        in_specs=[blk(w), full(2 * LANES),
                  pl.BlockSpec((1, LANES, s), lambda b, i: (b, 0, 0)),
                  blk(qi2.shape[2]), full(LANES),
                  pl.BlockSpec((1, 8, Q_BLOCK), lambda b, i: (b, 0, i))],
        out_specs=blk(w),
        out_shape=jax.ShapeDtypeStruct((bsz, s, w), F32),
        scratch_shapes=[pltpu.VMEM((s, LANES), F32),
                        pltpu.VMEM((n_heads * Q_BLOCK, 2 * LANES), BF16),
                        pltpu.VMEM((LANES, n_heads * Q_BLOCK), F32),
                        pltpu.VMEM((1, n_heads * Q_BLOCK), F32),
                        pltpu.VMEM((8, LANES), F32),
                        pltpu.VMEM((s, n_heads * Q_BLOCK), F32)],
        compiler_params=pltpu.CompilerParams(
            dimension_semantics=("parallel", "arbitrary"), vmem_limit_bytes=VMEM_LIMIT),
        name="dsa_attention",
    )(dq, kda, vd1t, qi2, ki1, wt)


def _ffn_kernel(x_ref, osb_ref, odsa_ref, mod_ref, gsb_ref, gdsa_ref, wo_ref, g_ref,
                wg_ref, wu_ref, wd_ref, fg_ref, fmod_ref, o_ref,
                x1_ref, h_ref, acc_ref, *, final):
    j = pl.program_id(2)

    @pl.when(j == 0)
    def _():
        mod = mod_ref[0]
        sb_w = osb_ref.shape[2]
        a = _rms(osb_ref[0], gsb_ref[...]).astype(BF16)
        b = _rms(odsa_ref[0], gdsa_ref[...]).astype(BF16)
        y = _dot(a, wo_ref[0:sb_w, :]) + _dot(b, wo_ref[sb_w:, :])
        x1 = x_ref[0] + mod[2:3] * y
        x1_ref[...] = x1
        h = _rms(x1, g_ref[...]) * (1.0 + mod[4:5]) + mod[3:4]
        h_ref[...] = h.astype(BF16)

    h = h_ref[...]
    g = _dot(h, wg_ref[...])
    u = _dot(h, wu_ref[...])
    act = (g / (1.0 + jnp.exp(-g))) * u
    part = _dot(act.astype(BF16), wd_ref[...])

    @pl.when(j == 0)
    def _():
        acc_ref[...] = part

    @pl.when(j > 0)
    def _():
        acc_ref[...] += part

    @pl.when(j == pl.num_programs(2) - 1)
    def _():
        y = x1_ref[...] + mod_ref[0][5:6] * acc_ref[...]
        if final:
            fmod = fmod_ref[0]
            y = _rms(y, fg_ref[...]) * (1.0 + fmod[1:2]) + fmod[0:1]
        o_ref[0] = y


def _out_ffn(x, osb, odsa, mod, gsb, gdsa, wo, gain, wg, wu, wd, fgain, fmod, *, final,
             tm=512, tf=1408):
    bsz, s, d = x.shape
    dff = wg.shape[1]
    sb_w, dsa_w = osb.shape[2], odsa.shape[2]
    tm = min(tm, s)
    row = lambda w: pl.BlockSpec((1, tm, w), lambda b, i, j: (b, i, 0))
    full = lambda shape: pl.BlockSpec(shape, lambda b, i, j: (0,) * len(shape))
    return pl.pallas_call(
        functools.partial(_ffn_kernel, final=final),
        grid=(bsz, s // tm, dff // tf),
        in_specs=[row(d), row(sb_w), row(dsa_w),
                  pl.BlockSpec((1, 6, d), lambda b, i, j: (b, 0, 0)),
                  full((1, sb_w)), full((1, dsa_w)), full((sb_w + dsa_w, d)), full((1, d)),
                  pl.BlockSpec((d, tf), lambda b, i, j: (0, j)),
                  pl.BlockSpec((d, tf), lambda b, i, j: (0, j)),
                  pl.BlockSpec((tf, d), lambda b, i, j: (j, 0)),
                  full((1, d)),
                  pl.BlockSpec((1, 2, d), lambda b, i, j: (b, 0, 0))],
        out_specs=row(d),
        out_shape=jax.ShapeDtypeStruct((bsz, s, d), F32),
        scratch_shapes=[pltpu.VMEM((tm, d), F32), pltpu.VMEM((tm, d), BF16),
                        pltpu.VMEM((tm, d), F32)],
        compiler_params=pltpu.CompilerParams(
            dimension_semantics=("parallel", "parallel", "arbitrary"),
            vmem_limit_bytes=VMEM_LIMIT),
        name="out_ffn",
    )(x, osb, odsa, mod, gsb, gdsa, wo, gain, wg, wu, wd, fgain, fmod)


def kernel(x, c, w_mod, b_mod, norm1_gain, norm2_gain, w_in, kv_gain, w_uk, w_uv, sb_out_gain,
           dsa_out_gain, w_o, w_gate, w_up, w_down, w_mod_final, b_mod_final, final_gain):
    bsz, s, d = x.shape
    depth = w_in.shape[0]
    sb_w = sb_out_gain.shape[1]
    dsa_w = dsa_out_gain.shape[1]
    kv_w = kv_gain.shape[1]
    topk = min(TOPK_MAX, s // 4)
    assert s % max(SB_KC, DSA_KC) == 0 and topk % Q_BLOCK == 0
    assert HEAD_DIM * 2 == LANES and w_uk.shape[2] == HEAD_DIM and s // LANES <= 256

    mod = _mod(c, w_mod, b_mod).reshape(depth, bsz, 6, d)
    fmod = _mod(c, w_mod_final[None], b_mod_final[None]).reshape(bsz, 2, d)

    n_a = 3 * sb_w + dsa_w + kv_w
    n_idx = N_IDX_HEADS * IDX_DIM + IDX_DIM + N_IDX_HEADS
    idx_pad = N_IDX_HEADS * IDX_DIM + LANES - n_idx
    bkv = jnp.concatenate([jnp.zeros((3 * HEAD_DIM,), F32), jnp.ones((HEAD_DIM,), F32)])[None]

    for l in range(depth):
        wa = w_in[l][:, :n_a].astype(BF16)
        wi = jnp.pad(w_in[l][:, n_a:], ((0, 0), (0, idx_pad)))
        wih = wi.astype(BF16)
        wil = (wi - wih.astype(F32)).astype(BF16)
        wkv = jnp.concatenate([w_uk[l], w_uk[l], w_uv[l], jnp.zeros_like(w_uv[l])],
                              axis=1).astype(BF16)
        (sbq, sbk, sbv, dq, kda, qi2, ki1, wt, vd1t) = _in_proj(
            x, mod[l], norm1_gain[l][None], wa, wih, wil, kv_gain[l][None], wkv, bkv,
            sb_w=sb_w, dsa_w=dsa_w, kv_w=kv_w)
        osb = _sb_attention(sbq, sbk, sbv)
        odsa = _dsa_attention(dq, kda, vd1t, qi2, ki1, wt, topk=topk)
        x = _out_ffn(x, osb, odsa, mod[l], sb_out_gain[l][None], dsa_out_gain[l][None],
                     w_o[l].astype(BF16), norm2_gain[l][None], w_gate[l].astype(BF16),
                     w_up[l].astype(BF16), w_down[l].astype(BF16), final_gain[None], fmod,
                     final=(l == depth - 1))
    return x
```

```python
import functools

import jax
import jax.numpy as jnp
from jax import lax
from jax.experimental import pallas as pl
from jax.experimental.pallas import tpu as pltpu

F32 = jnp.float32
BF16 = jnp.bfloat16
I32 = jnp.int32

HEAD_DIM = 64
LANES = 128
Q_BLOCK = 128
N_IDX_HEADS = 4
IDX_DIM = 64
TOPK_MAX = 256
RMS_EPS = 1e-6
INT_MIN = -2 ** 31
NEG_BIG = -1e30
VMEM_LIMIT = 52 * 1024 * 1024
SB_KC = 512
SB_STEP = 2
SB_ROW_TILE = 64
SB_UNDERFLOW = 110.0
DSA_KC = 512


def _nt(a, b):
    return lax.dot_general(a, b, (((1,), (1,)), ((), ())), preferred_element_type=F32)


def _dot(a, b):
    return jnp.dot(a, b, preferred_element_type=F32)


def _split(x):
    hi = x.astype(BF16)
    lo = (x - hi.astype(F32)).astype(BF16)
    return hi, lo


def _rms(x, gain):
    ms = jnp.mean(x * x, axis=-1, keepdims=True)
    return x * lax.rsqrt(ms + RMS_EPS) * gain


def _mod_kernel(c_ref, w_ref, b_ref, o_ref):
    c = c_ref[...]
    ca = c / (1.0 + jnp.exp(-c))
    ch, cl = _split(ca)
    wh, wl = _split(w_ref[0])
    o_ref[0] = _dot(ch, wh) + _dot(ch, wl) + _dot(cl, wh) + b_ref[0]


def _mod(c, w, b, tn=1024):
    nl, d, n = w.shape
    bsz = c.shape[0]
    return pl.pallas_call(
        _mod_kernel,
        grid=(nl, n // tn),
        in_specs=[pl.BlockSpec((bsz, d), lambda l, j: (0, 0)),
                  pl.BlockSpec((1, d, tn), lambda l, j: (l, 0, j)),
                  pl.BlockSpec((1, 1, tn), lambda l, j: (l, 0, j))],
        out_specs=pl.BlockSpec((1, bsz, tn), lambda l, j: (l, 0, j)),
        out_shape=jax.ShapeDtypeStruct((nl, bsz, n), F32),
        name="adaln_mod",
    )(c, w, b.reshape(nl, 1, n))


def _in_kernel(x_ref, mod_ref, g_ref, wa_ref, wih_ref, wil_ref, kvg_ref, wkv_ref, bkv_ref,
               sbq_ref, sbk_ref, sbv_ref, dq_ref, kda_ref, qi2_ref, ki1_ref, wt_ref, vd1t_ref,
               *, sb_w, dsa_w, kv_w):
    tm = x_ref.shape[1]
    x = x_ref[0]
    mod = mod_ref[0]
    h = _rms(x, g_ref[...]) * (1.0 + mod[1:2]) + mod[0:1]
    hh, hl = _split(h)
    main = _dot(hh, wa_ref[...])
    scale = HEAD_DIM ** -0.5
    sbq_ref[0] = (main[:, 0:sb_w] * scale).astype(BF16)
    sbk_ref[0] = main[:, sb_w:2 * sb_w].astype(BF16)
    sbv_ref[0] = main[:, 2 * sb_w:3 * sb_w].astype(BF16)
    o = 3 * sb_w
    dq_ref[0] = (main[:, o:o + dsa_w] * scale).astype(BF16)
    o += dsa_w
    kv = _rms(main[:, o:o + kv_w], kvg_ref[...])
    kvp = _dot(kv.astype(BF16), wkv_ref[...]) + bkv_ref[...]
    lane = lax.broadcasted_iota(I32, (tm, LANES), 1)
    pos = pl.program_id(1) * tm + lax.broadcasted_iota(I32, (tm, LANES), 0)
    posf = jnp.where(lane == 0, pos >> 7, jnp.where(lane == 1, pos & (LANES - 1), 0)).astype(F32)
    kda_ref[0, :, 0:LANES] = kvp[:, 0:LANES].astype(BF16)
    kda_ref[0, :, LANES:2 * LANES] = posf.astype(BF16)
    vd1t_ref[0] = kvp[:, LANES:2 * LANES].T.astype(BF16)
    wih = wih_ref[...]
    idx = _dot(hh, wih) + _dot(hh, wil_ref[...]) + _dot(hl, wih)
    first = lane < IDX_DIM
    for pr in range(N_IDX_HEADS // 2):
        pair = idx[:, pr * LANES:(pr + 1) * LANES] * (IDX_DIM ** -0.5)
        rolled = pltpu.roll(pair, IDX_DIM, 1)
        for half in range(2):
            dup = jnp.where(first, pair, rolled) if half == 0 else jnp.where(first, rolled, pair)
            qh, ql = _split(dup)
            hd = 2 * pr + half
            qi2_ref[0, :, 2 * hd * LANES:(2 * hd + 1) * LANES] = qh
            qi2_ref[0, :, (2 * hd + 1) * LANES:(2 * hd + 2) * LANES] = jnp.where(
                first, ql, jnp.zeros_like(ql))
    nq = N_IDX_HEADS * IDX_DIM
    kw = idx[:, nq:nq + LANES]
    kh, kl = _split(jnp.where(first, kw, pltpu.roll(kw, IDX_DIM, 1)))
    ki1_ref[0] = jnp.where(first, kh, kl)
    kwt = kw.T
    wt_ref[0] = kwt[IDX_DIM:IDX_DIM + 8, :] * (N_IDX_HEADS ** -0.5)


def _in_proj(x, mod, gain, wa, wih, wil, kvg, wkv, bkv, *, sb_w, dsa_w, kv_w, tm=512):
    bsz, s, d = x.shape
    na = wa.shape[1]
    ni = wih.shape[1]
    nkv = wkv.shape[1]
    full = lambda shape: pl.BlockSpec(shape, lambda b, i: (0,) * len(shape))
    row = lambda w: pl.BlockSpec((1, tm, w), lambda b, i: (b, i, 0))
    outs = [
        (sb_w, BF16), (sb_w, BF16), (sb_w, BF16), (dsa_w, BF16),
        (2 * LANES, BF16), (2 * LANES * N_IDX_HEADS, BF16), (LANES, BF16),
    ]
    out_shape = [jax.ShapeDtypeStruct((bsz, s, w), dt) for w, dt in outs]
    out_specs = [row(w) for w, _ in outs]
    for rows, dt in ((8, F32), (LANES, BF16)):
        out_shape.append(jax.ShapeDtypeStruct((bsz, rows, s), dt))
        out_specs.append(pl.BlockSpec((1, rows, tm), lambda b, i: (b, 0, i)))
    return pl.pallas_call(
        functools.partial(_in_kernel, sb_w=sb_w, dsa_w=dsa_w, kv_w=kv_w),
        grid=(bsz, s // tm),
        in_specs=[row(d),
                  pl.BlockSpec((1, 6, d), lambda b, i: (b, 0, 0)),
                  full((1, d)), full((d, na)), full((d, ni)), full((d, ni)),
                  full((1, kv_w)), full((kv_w, nkv)), full((1, nkv))],
        out_specs=out_specs,
        out_shape=out_shape,
        compiler_params=pltpu.CompilerParams(
            dimension_semantics=("parallel", "parallel"), vmem_limit_bytes=VMEM_LIMIT),
        name="in_proj",
    )(x, mod, gain, wa, wih, wil, kvg, wkv, bkv)


def _sb_kernel(q_ref, k_ref, v_ref, o_ref, car_ref, *, npair):
    i = pl.program_id(1)
    t = Q_BLOCK
    kc = SB_KC
    nsub = kc // LANES
    lane = lax.broadcasted_iota(I32, (t, LANES), 1)
    first = lane < HEAD_DIM
    zb = jnp.zeros((t, LANES), BF16)
    r2 = lax.broadcasted_iota(I32, (LANES, 2 * LANES), 0)
    c2 = lax.broadcasted_iota(I32, (LANES, 2 * LANES), 1)
    uo = jnp.where((c2 >= LANES) | (r2 > c2), 1.0, 0.0).astype(BF16)
    ct = i // nsub
    rt = SB_ROW_TILE

    def stage_scores(pr, start, ns):
        cols = slice(pr * LANES, (pr + 1) * LANES)
        q2 = q_ref[0, :, cols]
        qs = jnp.concatenate([jnp.where(first, q2, zb), jnp.where(first, zb, q2)], axis=0)
        z = _nt(qs, k_ref[0, pl.ds(start, ns * LANES), cols])
        return jnp.concatenate([z[:, j * LANES:(j + 1) * LANES] for j in range(ns)], axis=0)

    def causal_tile(r0):
        rr = (r0 + lax.broadcasted_iota(I32, (rt, LANES), 0)) & (t - 1)
        return lax.broadcasted_iota(I32, (rt, LANES), 1) < rr

    def stage_cumsum(u, top, ns):
        ds, sps = [], []
        for r0 in range(0, ns * 2 * t, rt):
            ut = u[r0:r0 + rt]
            sp = jnp.maximum(ut, 0.0) + jnp.log(1.0 + jnp.exp(-jnp.abs(ut)))
            ds.append(ut - sp)
            if top and r0 >= (ns - 1) * 2 * t:
                sp = jnp.where(causal_tile(r0), sp, 0.0)
            sps.append(sp.astype(BF16))
        spb = jnp.concatenate(sps, axis=0)
        half = ns * t
        cs = jnp.concatenate([_dot(spb[:half], uo), _dot(spb[half:], uo)], axis=0)
        return ds, cs

    def stage_output(pr, start, ds, cs, top, ns):
        cols = slice(pr * LANES, (pr + 1) * LANES)
        v2 = v_ref[0, pl.ds(start, ns * LANES), cols]
        firstk = lax.broadcasted_iota(I32, (ns * LANES, LANES), 1) < HEAD_DIM
        zk = jnp.zeros((ns * LANES, LANES), BF16)
        ntile = 2 * t // rt
        runs = [jnp.zeros((rt, LANES), F32) if top else car_ref[pr, k * rt:(k + 1) * rt]
                for k in range(ntile)]
        pieces = [[None] * ntile for _ in range(ns)]
        for j in reversed(range(ns)):
            for k in range(ntile):
                r0 = j * 2 * t + k * rt
                a = jnp.exp(ds[r0 // rt] - (cs[r0:r0 + rt, :LANES] + runs[k]))
                if top and j == ns - 1:
                    a = jnp.where(causal_tile(r0), a, 0.0)
                pieces[j][k] = a.astype(BF16)
                runs[k] = runs[k] + cs[r0:r0 + rt, LANES:]
        for k in range(ntile):
            car_ref[pr, k * rt:(k + 1) * rt] = runs[k]
        pieces = [jnp.concatenate(p, axis=0) for p in pieces]
        a_e = jnp.concatenate([p[:t] for p in pieces], axis=1)
        a_o = jnp.concatenate([p[t:] for p in pieces], axis=1)
        pv = _dot(a_e, jnp.where(firstk, v2, zk)) + _dot(a_o, jnp.where(firstk, zk, v2))
        if top:
            o_ref[0, :, cols] = pv
        else:
            o_ref[0, :, cols] += pv
        return functools.reduce(jnp.minimum, runs)

    def all_pairs(start, top, ns):
        us, mids, mins = {}, {}, []
        for step in range(npair + 2):
            if step < npair:
                us[step] = stage_scores(step, start, ns)
            if 0 <= step - 1 < npair:
                mids[step - 1] = stage_cumsum(us.pop(step - 1), top, ns)
            if 0 <= step - 2 < npair:
                mins.append(stage_output(step - 2, start, *mids.pop(step - 2), top, ns))
        return functools.reduce(jnp.minimum, mins)

    top_start = pl.multiple_of(ct * kc, kc)
    for r in range(nsub):
        @pl.when(i % nsub == r)
        def _(r=r):
            all_pairs(top_start, True, r + 1)

    step_keys = SB_STEP * LANES
    n_steps = ct * (kc // step_keys)

    def cond(state):
        j, min_carry = state
        return (j < n_steps) & (min_carry < SB_UNDERFLOW)

    def body(state):
        j, _ = state
        start = pl.multiple_of(top_start - (j + 1) * step_keys, step_keys)
        return j + 1, jnp.min(all_pairs(start, False, SB_STEP))

    lax.while_loop(cond, body, (jnp.int32(0), jnp.min(car_ref[...])))


def _sb_attention(q, k, v):
    bsz, s, w = q.shape
    npair = w // LANES
    nq = s // Q_BLOCK
    return pl.pallas_call(
        functools.partial(_sb_kernel, npair=npair),
        grid=(bsz, nq),
        in_specs=[pl.BlockSpec((1, Q_BLOCK, w), lambda b, i: (b, i, 0)),
                  pl.BlockSpec((1, s, w), lambda b, i: (b, 0, 0)),
                  pl.BlockSpec((1, s, w), lambda b, i: (b, 0, 0))],
        out_specs=pl.BlockSpec((1, Q_BLOCK, w), lambda b, i: (b, i, 0)),
        out_shape=jax.ShapeDtypeStruct((bsz, s, w), F32),
        scratch_shapes=[pltpu.VMEM((npair, 2 * Q_BLOCK, LANES), F32)],
        compiler_params=pltpu.CompilerParams(
            dimension_semantics=("parallel", "arbitrary"), vmem_limit_bytes=VMEM_LIMIT),
        name="sb_attention",
    )(q, k, v)


def _key_to_float(k):
    return lax.bitcast_convert_type(k ^ ((k >> 31) & 0x7FFFFFFF), F32)


def _dsa_kernel(dq_ref, kda_ref, vd1t_ref, qi2_ref, ki1_ref, wt_ref, o_ref,
                key_ref, qa_ref, acc_ref, m_ref, thr_ref, sc_ref, *, topk, n_heads):
    i = pl.program_id(1)
    t = Q_BLOCK
    kc = DSA_KC
    nsub = kc // t
    n_full = (i + 1) // nsub
    rem = (i + 1) % nsub
    n_kc = n_full + jnp.minimum(rem, 1)
    kf = float(topk)
    lane = lax.broadcasted_iota(I32, (t, LANES), 1)
    row = lax.broadcasted_iota(I32, (t, LANES), 0)
    first = lane < HEAD_DIM

    zb = jnp.zeros((t, LANES), BF16)
    for hd in range(n_heads):
        q2 = dq_ref[0, :, (hd // 2) * LANES:(hd // 2 + 1) * LANES]
        slope = 2.0 ** (-8.0 * (hd + 1) / n_heads)
        aug = jnp.where(lane == 0, LANES * slope, jnp.where(lane == 1, slope, 0.0)).astype(BF16)
        qa_ref[hd * t:(hd + 1) * t, 0:LANES] = (
            jnp.where(first, q2, zb) if hd % 2 == 0 else jnp.where(first, zb, q2))
        qa_ref[hd * t:(hd + 1) * t, LANES:2 * LANES] = aug

    wt = wt_ref[0]
    rowlane = (lax.broadcasted_iota(I32, (kc, LANES), 0)
               - lax.broadcasted_iota(I32, (kc, LANES), 1))

    def score_rows(start, rows):
        sc_ref[pl.ds(start, rows), :] = _nt(kda_ref[0, pl.ds(start, rows), :], qa_ref[...])
        k1 = ki1_ref[0, pl.ds(start, rows), :]
        lhs = jnp.concatenate([k1, k1], axis=1)
        score = None
        for pr in range(N_IDX_HEADS // 2):
            w2 = jnp.concatenate(
                [qi2_ref[0, :, 2 * hd * LANES:(2 * hd + 2) * LANES] for hd in (2 * pr, 2 * pr + 1)],
                axis=0)
            lg = _nt(lhs, w2)
            for half in range(2):
                hd = 2 * pr + half
                term = jnp.maximum(lg[:, half * LANES:(half + 1) * LANES], 0.0) * wt[hd:hd + 1, :]
                score = term if score is None else score + term
        key_ref[pl.ds(start, rows), :] = jnp.where(rowlane[:rows] > i * t - start, -jnp.inf, score)

    def score_chunk(c, carry):
        score_rows(pl.multiple_of(c * kc, kc), kc)
        return carry

    lax.fori_loop(0, n_full, score_chunk, 0)
    rem_start = pl.multiple_of(n_full * kc, kc)
    for r in range(1, nsub):
        @pl.when(rem == r)
        def _(r=r):
            score_rows(rem_start, r * t)
            key_ref[pl.ds(rem_start + r * t, kc - r * t), :] = jnp.full(
                (kc - r * t, LANES), -jnp.inf, F32)

    searching = (i + 1) * t > topk

    def search(n128):
        pieces = [(lo_r, min(lo_r + kc, n128 * t)) for lo_r in range(0, n128 * t, kc)]

        def bit_step(bi, state):
            lo, cnt_lo = state
            cand = lo + lax.shift_left(jnp.int32(1), 31 - bi)
            cand_f = _key_to_float(cand)
            acc = None
            for lo_r, hi_r in pieces:
                m = jnp.where(key_ref[lo_r:hi_r, :] >= cand_f, 1.0, 0.0)
                m = m.reshape(8, (hi_r - lo_r) // 8, LANES).sum(axis=0)
                m = m.reshape((hi_r - lo_r) // 64, 8, LANES).sum(axis=0)
                acc = m if acc is None else acc + m
            cnt = jnp.sum(acc, axis=0, keepdims=True)
            take = cnt >= kf
            return jnp.where(take, cand, lo), jnp.where(take, cnt, cnt_lo)

        lo, cnt_lo = lax.fori_loop(0, 32, bit_step, (jnp.full((1, LANES), INT_MIN, I32),
                                                     jnp.full((1, LANES), kf, F32)), unroll=4)
        return _key_to_float(lo), cnt_lo

    thr_ref[...] = jnp.full(thr_ref.shape, jnp.finfo(F32).min, F32)
    for n128 in range(topk // t + 1, key_ref.shape[0] // t + 1):
        @pl.when(i + 1 == n128)
        def _(n128=n128):
            thr_f, cnt_f = search(n128)
            thr_ref[0:1, :] = thr_f
            thr_ref[1:2, :] = cnt_f

    thr = thr_ref[0:1, :]
    cnt_ge = thr_ref[1:2, :]

    @pl.when((jnp.max(cnt_ge) > kf) & searching)
    def _():
        excess = cnt_ge - kf
        su = jnp.where(lane > row, 1.0, 0.0).astype(BF16)

        def body(jj, after):
            start = pl.multiple_of((n_kc - 1 - jj) * kc, kc)
            subs = [key_ref[pl.ds(start + j * t, t), :] for j in range(nsub)]
            eqfs = [jnp.where(sub == thr, 1.0, 0.0) for sub in subs]
            ranks = [_dot(su, eqf.astype(BF16)) for eqf in eqfs]
            for j in reversed(range(nsub)):
                demote = (subs[j] == thr) & (ranks[j] + after < excess)
                key_ref[pl.ds(start + j * t, t), :] = jnp.where(demote, -jnp.inf, subs[j])
                after = after + jnp.sum(eqfs[j], axis=0, keepdims=True)
            return after

        lax.fori_loop(0, n_kc, body, jnp.zeros((1, LANES), F32))

    m_ref[...] = jnp.full(m_ref.shape, NEG_BIG, F32)
    acc_ref[...] = jnp.zeros(acc_ref.shape, F32)

    def attn_rows(start, rows):
        vt = vd1t_ref[0, :, pl.ds(start, rows)]
        bias = jnp.where(key_ref[pl.ds(start, rows), :] >= thr, 0.0, -jnp.inf)
        sc = sc_ref[pl.ds(start, rows), :] + jnp.concatenate([bias] * n_heads, axis=1)
        m_old = m_ref[...]
        m_new = jnp.maximum(m_old, jnp.max(sc, axis=0, keepdims=True))
        acc_ref[...] = (acc_ref[...] * jnp.exp(m_old - m_new)
                        + _dot(vt, jnp.exp(sc - m_new).astype(BF16)))
        m_ref[...] = m_new

    def attn_chunk(c, carry):
        attn_rows(pl.multiple_of(c * kc, kc), kc)
        return carry

    lax.fori_loop(0, n_full, attn_chunk, 0)
    for r in range(1, nsub):
        @pl.when(rem == r)
        def _(r=r):
            attn_rows(rem_start, r * t)

    for pr in range(n_heads // 2):
        a_e = acc_ref[:, (2 * pr) * t:(2 * pr + 1) * t]
        a_o = acc_ref[:, (2 * pr + 1) * t:(2 * pr + 2) * t]
        r = jnp.concatenate([a_e[:HEAD_DIM] / a_e[HEAD_DIM:], a_o[:HEAD_DIM] / a_o[HEAD_DIM:]], axis=0)
        o_ref[0, :, pr * LANES:(pr + 1) * LANES] = r.T


def _dsa_attention(dq, kda, vd1t, qi2, ki1, wt, *, topk):
    bsz, s, w = dq.shape
    nq = s // Q_BLOCK
    n_heads = w // HEAD_DIM
    blk = lambda wd: pl.BlockSpec((1, Q_BLOCK, wd), lambda b, i: (b, i, 0))
    full = lambda wd: pl.BlockSpec((1, s, wd), lambda b, i: (b, 0, 0))
    return pl.pallas_call(
        functools.partial(_dsa_kernel, topk=topk, n_heads=n_heads),
        grid=(bsz, nq),
        in_specs=[blk(w), full(2 * LANES),
                  pl.BlockSpec((1, LANES, s), lambda b, i: (b, 0, 0)),
                  blk(qi2.shape[2]), full(LANES),
                  pl.BlockSpec((1, 8, Q_BLOCK), lambda b, i: (b, 0, i))],
        out_specs=blk(w),
        out_shape=jax.ShapeDtypeStruct((bsz, s, w), F32),
        scratch_shapes=[pltpu.VMEM((s, LANES), F32),
                        pltpu.VMEM((n_heads * Q_BLOCK, 2 * LANES), BF16),
                        pltpu.VMEM((LANES, n_heads * Q_BLOCK), F32),
                        pltpu.VMEM((1, n_heads * Q_BLOCK), F32),
                        pltpu.VMEM((8, LANES), F32),
                        pltpu.VMEM((s, n_heads * Q_BLOCK), F32)],
        compiler_params=pltpu.CompilerParams(
            dimension_semantics=("parallel", "arbitrary"), vmem_limit_bytes=VMEM_LIMIT),
        name="dsa_attention",
    )(dq, kda, vd1t, qi2, ki1, wt)


def _ffn_kernel(x_ref, osb_ref, odsa_ref, mod_ref, gsb_ref, gdsa_ref, wo_ref, g_ref,
                wg_ref, wu_ref, wd_ref, fg_ref, fmod_ref, o_ref,
                x1_ref, h_ref, acc_ref, *, final):
    j = pl.program_id(2)

    @pl.when(j == 0)
    def _():
        mod = mod_ref[0]
        sb_w = osb_ref.shape[2]
        a = _rms(osb_ref[0], gsb_ref[...]).astype(BF16)
        b = _rms(odsa_ref[0], gdsa_ref[...]).astype(BF16)
        y = _dot(a, wo_ref[0:sb_w, :]) + _dot(b, wo_ref[sb_w:, :])
        x1 = x_ref[0] + mod[2:3] * y
        x1_ref[...] = x1
        h = _rms(x1, g_ref[...]) * (1.0 + mod[4:5]) + mod[3:4]
        h_ref[...] = h.astype(BF16)

    h = h_ref[...]
    g = _dot(h, wg_ref[...])
    u = _dot(h, wu_ref[...])
    act = (g / (1.0 + jnp.exp(-g))) * u
    part = _dot(act.astype(BF16), wd_ref[...])

    @pl.when(j == 0)
    def _():
        acc_ref[...] = part

    @pl.when(j > 0)
    def _():
        acc_ref[...] += part

    @pl.when(j == pl.num_programs(2) - 1)
    def _():
        y = x1_ref[...] + mod_ref[0][5:6] * acc_ref[...]
        if final:
            fmod = fmod_ref[0]
            y = _rms(y, fg_ref[...]) * (1.0 + fmod[1:2]) + fmod[0:1]
        o_ref[0] = y


def _out_ffn(x, osb, odsa, mod, gsb, gdsa, wo, gain, wg, wu, wd, fgain, fmod, *, final,
             tm=512, tf=1408):
    bsz, s, d = x.shape
    dff = wg.shape[1]
    sb_w, dsa_w = osb.shape[2], odsa.shape[2]
    tm = min(tm, s)
    row = lambda w: pl.BlockSpec((1, tm, w), lambda b, i, j: (b, i, 0))
    full = lambda shape: pl.BlockSpec(shape, lambda b, i, j: (0,) * len(shape))
    return pl.pallas_call(
        functools.partial(_ffn_kernel, final=final),
        grid=(bsz, s // tm, dff // tf),
        in_specs=[row(d), row(sb_w), row(dsa_w),
                  pl.BlockSpec((1, 6, d), lambda b, i, j: (b, 0, 0)),
                  full((1, sb_w)), full((1, dsa_w)), full((sb_w + dsa_w, d)), full((1, d)),
                  pl.BlockSpec((d, tf), lambda b, i, j: (0, j)),
                  pl.BlockSpec((d, tf), lambda b, i, j: (0, j)),
                  pl.BlockSpec((tf, d), lambda b, i, j: (j, 0)),
                  full((1, d)),
                  pl.BlockSpec((1, 2, d), lambda b, i, j: (b, 0, 0))],
        out_specs=row(d),
        out_shape=jax.ShapeDtypeStruct((bsz, s, d), F32),
        scratch_shapes=[pltpu.VMEM((tm, d), F32), pltpu.VMEM((tm, d), BF16),
                        pltpu.VMEM((tm, d), F32)],
        compiler_params=pltpu.CompilerParams(
            dimension_semantics=("parallel", "parallel", "arbitrary"),
            vmem_limit_bytes=VMEM_LIMIT),
        name="out_ffn",
    )(x, osb, odsa, mod, gsb, gdsa, wo, gain, wg, wu, wd, fgain, fmod)


def kernel(x, c, w_mod, b_mod, norm1_gain, norm2_gain, w_in, kv_gain, w_uk, w_uv, sb_out_gain,
           dsa_out_gain, w_o, w_gate, w_up, w_down, w_mod_final, b_mod_final, final_gain):
    bsz, s, d = x.shape
    depth = w_in.shape[0]
    sb_w = sb_out_gain.shape[1]
    dsa_w = dsa_out_gain.shape[1]
    kv_w = kv_gain.shape[1]
    topk = min(TOPK_MAX, s // 4)
    assert s % max(SB_KC, DSA_KC) == 0 and topk % Q_BLOCK == 0
    assert HEAD_DIM * 2 == LANES and w_uk.shape[2] == HEAD_DIM and s // LANES <= 256

    mod = _mod(c, w_mod, b_mod).reshape(depth, bsz, 6, d)
    fmod = _mod(c, w_mod_final[None], b_mod_final[None]).reshape(bsz, 2, d)

    n_a = 3 * sb_w + dsa_w + kv_w
    n_idx = N_IDX_HEADS * IDX_DIM + IDX_DIM + N_IDX_HEADS
    idx_pad = N_IDX_HEADS * IDX_DIM + LANES - n_idx
    bkv = jnp.concatenate([jnp.zeros((3 * HEAD_DIM,), F32), jnp.ones((HEAD_DIM,), F32)])[None]

    for l in range(depth):
        wa = w_in[l][:, :n_a].astype(BF16)
        wi = jnp.pad(w_in[l][:, n_a:], ((0, 0), (0, idx_pad)))
        wih = wi.astype(BF16)
        wil = (wi - wih.astype(F32)).astype(BF16)
        wkv = jnp.concatenate([w_uk[l], w_uk[l], w_uv[l], jnp.zeros_like(w_uv[l])],
                              axis=1).astype(BF16)
        (sbq, sbk, sbv, dq, kda, qi2, ki1, wt, vd1t) = _in_proj(
            x, mod[l], norm1_gain[l][None], wa, wih, wil, kv_gain[l][None], wkv, bkv,
            sb_w=sb_w, dsa_w=dsa_w, kv_w=kv_w)
        osb = _sb_attention(sbq, sbk, sbv)
        odsa = _dsa_attention(dq, kda, vd1t, qi2, ki1, wt, topk=topk)
        x = _out_ffn(x, osb, odsa, mod[l], sb_out_gain[l][None], dsa_out_gain[l][None],
                     w_o[l].astype(BF16), norm2_gain[l][None], w_gate[l].astype(BF16),
                     w_up[l].astype(BF16), w_down[l].astype(BF16), final_gain[None], fmod,
                     final=(l == depth - 1))
    return x
```
